```python
import math
import jax, jax.numpy as jnp
from jax import lax
import numpy as np

D_MODEL = 1024
BATCH = 4
SEQ = 8192
DEPTH = 2

N_MIXERS = 2
N_POOL_LAYERS = (DEPTH + 1) // 2
N_MOBA_LAYERS = DEPTH // 2
POOL_WINDOWS = (2, 4, 8, 16)
N_POOL_GROUPS = len(POOL_WINDOWS)
POOL_GROUP = D_MODEL // N_POOL_GROUPS
HEAD_DIM = 64
N_HEADS = D_MODEL // HEAD_DIM
MOBA_BLOCK = 256
MOBA_TOPK = 3
Q_CHUNK = 128
NUM_BUCKETS = 32
MAX_DISTANCE = 1024
D_FF = 4 * D_MODEL
EPS = 1e-6

kernel_name = "hybrid_pool_moba_adaln_block"


def rmsnorm(x, g):
    xf = x.astype(jnp.float32)
    y = xf * lax.rsqrt(jnp.mean(xf * xf, axis=-1, keepdims=True) + EPS)
    return (y * g.astype(jnp.float32)).astype(x.dtype)


def modulate(h, shift, scale):
    return h * (1 + scale[:, None, :]) + shift[:, None, :]


def t5_bucket(dist):
    max_exact = NUM_BUCKETS // 2
    nf = jnp.maximum(dist, 1).astype(jnp.float32)
    large = max_exact + (jnp.log(nf / max_exact) / math.log(MAX_DISTANCE / max_exact)
                         * (NUM_BUCKETS - max_exact)).astype(jnp.int32)
    large = jnp.minimum(large, NUM_BUCKETS - 1)
    return jnp.where(dist < max_exact, dist, large)


def pool_mixer(h, w_pool, pool_scale):
    B, S, D = h.shape
    hf = h.astype(jnp.float32)
    cs = jnp.concatenate([jnp.zeros((B, 1, D), jnp.float32), jnp.cumsum(hf, axis=1)], axis=1)
    t = jnp.arange(S)
    groups = []
    for g, w in enumerate(POOL_WINDOWS):
        sl = slice(g * POOL_GROUP, (g + 1) * POOL_GROUP)
        cs_g = cs[:, :, sl]
        lo = jnp.maximum(t + 1 - w, 0)
        win_sum = cs_g[:, 1:] - jnp.take(cs_g, lo, axis=1)
        cnt = jnp.minimum(t + 1, w).astype(jnp.float32)[None, :, None]
        groups.append(win_sum / cnt - hf[:, :, sl])
    pooled = jnp.stack(groups, axis=2)
    y = jnp.einsum('bsgi,gio->bsgo', pooled, w_pool.astype(jnp.float32)).reshape(B, S, D)
    return (y * pool_scale.astype(jnp.float32)).astype(h.dtype)


def moba_attention(h, w_qkv, w_o, rel_bias):
    B, S, D = h.shape
    H, Dh, BL = N_HEADS, HEAD_DIM, MOBA_BLOCK
    qkv = h @ w_qkv
    q, k, v = jnp.split(qkv, 3, axis=-1)
    q = q.reshape(B, S, H, Dh).transpose(0, 2, 1, 3)
    k = k.reshape(B, S, H, Dh).transpose(0, 2, 1, 3)
    v = v.reshape(B, S, H, Dh).transpose(0, 2, 1, 3)
    nb = -(-S // BL)
    pad = nb * BL - S
    kb = jnp.pad(k, ((0, 0), (0, 0), (0, pad), (0, 0))).reshape(B, H, nb, BL, Dh)
    vb = jnp.pad(v, ((0, 0), (0, 0), (0, pad), (0, 0))).reshape(B, H, nb, BL, Dh)
    kmean = jnp.mean(kb.astype(jnp.float32), axis=3)
    top = min(MOBA_TOPK, nb - 1)
    scale = HEAD_DIM ** -0.5
    rb = rel_bias.astype(jnp.float32)
    rbT = rb.T
    bi = jnp.arange(B)[:, None, None]
    hi = jnp.arange(H)[None, :, None]
    hi4 = jnp.arange(H)[None, :, None, None]
    n_chunks = S // Q_CHUNK

    def chunk_fn(ci):
        s0 = ci * Q_CHUNK
        qf = lax.dynamic_slice_in_dim(q, s0, Q_CHUNK, axis=2).astype(jnp.float32)
        tq = s0 + jnp.arange(Q_CHUNK)
        own = s0 // BL
        k_own = lax.dynamic_index_in_dim(kb, own, axis=2, keepdims=False)
        v_own = lax.dynamic_index_in_dim(vb, own, axis=2, keepdims=False)
        dist_own = tq[:, None] - (own * BL + jnp.arange(BL))[None, :]
        bias_own = rb[t5_bucket(jnp.maximum(dist_own, 0))].transpose(2, 0, 1)[None]
        logit_own = jnp.einsum('bhqd,bhkd->bhqk', qf, k_own.astype(jnp.float32)) * scale + bias_own
        logit_own = jnp.where((dist_own >= 0)[None, None], logit_own, -jnp.inf)
        logits = []
        sel = []
        if top > 0:
            gate = jnp.einsum('bhqd,bhnd->bhqn', qf, kmean)
            past = jnp.arange(nb) < own
            gate = jnp.where(past[None, None, None], gate, -jnp.inf)
            _, idx = lax.top_k(gate, top)
            for r in range(top):
                idx_r = idx[..., r]
                k_sel = kb[bi, hi, idx_r]
                kpos = idx_r[..., None] * BL + jnp.arange(BL)
                dist = tq[None, None, :, None] - kpos
                bias_r = rbT[hi4, t5_bucket(jnp.maximum(dist, 0))]
                lg = jnp.einsum('bhqd,bhqkd->bhqk', qf, k_sel.astype(jnp.float32)) * scale + bias_r
                lg = jnp.where((idx_r < own)[..., None], lg, -jnp.inf)
                logits.append(lg)
                sel.append(idx_r)
        logits.append(logit_own)
        p = jax.nn.softmax(jnp.concatenate(logits, axis=-1), axis=-1)
        out = jnp.einsum('bhqk,bhkd->bhqd', p[..., top * BL:], v_own.astype(jnp.float32))
        for r in range(top):
            v_sel = vb[bi, hi, sel[r]]
            out = out + jnp.einsum('bhqk,bhqkd->bhqd', p[..., r * BL:(r + 1) * BL],
                                   v_sel.astype(jnp.float32))
        return out.astype(h.dtype)

    o = lax.map(chunk_fn, jnp.arange(n_chunks))
    o = o.transpose(1, 0, 3, 2, 4).reshape(B, S, D)
    return o @ w_o


def squared_relu_mlp(h, w_up, w_down):
    a = jax.nn.relu(h @ w_up)
    return (a * a) @ w_down


def setup_inputs(seed: int = 0) -> dict:
    key = jax.random.key(seed)
    ks = jax.random.split(key, 16)
    f32 = jnp.float32
    D = D_MODEL
    nrm = lambda k, shape, s: jax.random.normal(k, shape, f32) * s
    return {
        "x": nrm(ks[0], (BATCH, SEQ, D), 1.0),
        "c": nrm(ks[1], (BATCH, D), 1.0),
        "rel_bias": nrm(ks[2], (NUM_BUCKETS, N_HEADS), 0.5),
        "w_mod": nrm(ks[3], (DEPTH, D, 6 * D), 0.5 * D ** -0.5),
        "b_mod": nrm(ks[4], (DEPTH, 6 * D), 0.02),
        "norm_mix": 1.0 + nrm(ks[5], (DEPTH, D), 0.02),
        "norm_mlp": 1.0 + nrm(ks[6], (DEPTH, D), 0.02),
        "w_pool": nrm(ks[7], (N_POOL_LAYERS, N_POOL_GROUPS, POOL_GROUP, POOL_GROUP), POOL_GROUP ** -0.5),
        "pool_scale": 1.0 + nrm(ks[8], (N_POOL_LAYERS, D), 0.05),
        "w_qkv": nrm(ks[9], (N_MOBA_LAYERS, D, 3 * D), D ** -0.5),
        "w_o": nrm(ks[10], (N_MOBA_LAYERS, D, D), D ** -0.5),
        "w_up": nrm(ks[11], (DEPTH, D, D_FF), D ** -0.5),
        "w_down": nrm(ks[12], (DEPTH, D_FF, D), D_FF ** -0.5),
        "norm_final": 1.0 + nrm(ks[13], (D,), 0.02),
    }


def reference(x, c, rel_bias, w_mod, b_mod, norm_mix, norm_mlp, w_pool, pool_scale,
              w_qkv, w_o, w_up, w_down, norm_final):
    c_act = jax.nn.silu(c)
    for i in range(DEPTH):
        mod = c_act @ w_mod[i] + b_mod[i]
        sh1, sc1, g1, sh2, sc2, g2 = jnp.split(mod, 6, axis=-1)
        h = modulate(rmsnorm(x, norm_mix[i]), sh1, sc1)
        if i % N_MIXERS == 0:
            y = pool_mixer(h, w_pool[i // N_MIXERS], pool_scale[i // N_MIXERS])
        else:
            y = moba_attention(h, w_qkv[i // N_MIXERS], w_o[i // N_MIXERS], rel_bias)
        x = x + g1[:, None, :] * y
        h = modulate(rmsnorm(x, norm_mlp[i]), sh2, sc2)
        x = x + g2[:, None, :] * squared_relu_mlp(h, w_up[i], w_down[i])
    return rmsnorm(x, norm_final)
```

```python
import functools
import math

import jax
import jax.numpy as jnp
from jax import lax
from jax.experimental import pallas as pl
from jax.experimental.pallas import tpu as pltpu

F32 = jnp.float32
BF16 = jnp.bfloat16

HEAD_DIM = 64
MOBA_BLOCK = 256
MOBA_TOPK = 3
NUM_BUCKETS = 32
MAX_DISTANCE = 1024
POOL_WINDOWS = (2, 4, 8, 16)
EPS = 1e-6

V7X_LANES = 128
V7X_SUBLANES = 8
V7X_VMEM_LIMIT_BYTES = 56 * 1024 * 1024

NEAR_OFFSETS = 5
HEADS_PER_STEP = V7X_LANES // HEAD_DIM
POOL_HALO = max(POOL_WINDOWS)

POOL_ROWS = 512
MLP_ROWS = 512
MLP_FF_CHUNK = 512
QKV_ROWS = 512
MOD_COLS = 1536


def _params(semantics):
    return pltpu.CompilerParams(dimension_semantics=semantics,
                                vmem_limit_bytes=V7X_VMEM_LIMIT_BYTES)


def _dot(a, b):
    return jnp.dot(a, b, preferred_element_type=F32)


def _rms_modulate(x, gain, shift, scale):
    y = x * lax.rsqrt(jnp.mean(x * x, axis=-1, keepdims=True) + EPS)
    return (y * gain) * (1.0 + scale) + shift


def _mod_kernel(c_ref, w_ref, b_ref, o_ref):
    c = c_ref[...]
    a = c * (1.0 / (1.0 + jnp.exp(-c)))
    w = w_ref[0]
    a_hi = a.astype(BF16)
    a_lo = (a - a_hi.astype(F32)).astype(BF16)
    w_hi = w.astype(BF16)
    w_lo = (w - w_hi.astype(F32)).astype(BF16)
    acc = _dot(a_hi, w_hi) + _dot(a_hi, w_lo) + _dot(a_lo, w_hi)
    o_ref[0] = acc + b_ref[0]


def _modulation(c, w_mod, b_mod):
    depth, d, n = w_mod.shape
    b = c.shape[0]
    rows = -(-b // V7X_SUBLANES) * V7X_SUBLANES
    c_pad = jnp.pad(c, ((0, rows - b), (0, 0)))
    out = pl.pallas_call(
        _mod_kernel,
        grid=(depth, n // MOD_COLS),
        in_specs=[
            pl.BlockSpec((rows, d), lambda i, j: (0, 0)),
            pl.BlockSpec((1, d, MOD_COLS), lambda i, j: (i, 0, j)),
            pl.BlockSpec((1, 1, MOD_COLS), lambda i, j: (i, 0, j)),
        ],
        out_specs=pl.BlockSpec((1, rows, MOD_COLS), lambda i, j: (i, 0, j)),
        out_shape=jax.ShapeDtypeStruct((depth, rows, n), F32),
        compiler_params=_params(("parallel", "parallel")),
        name="modulation",
    )(c_pad, w_mod, b_mod.reshape(depth, 1, n))
    return out[:, :b]


def _pool_kernel(x_ref, halo_ref, gain_ref, sh_ref, sc_ref, gate_ref, w_ref, ps_ref, o_ref):
    t = pl.program_id(1)
    rows = x_ref.shape[1]
    group = w_ref.shape[1]
    x = x_ref[0]
    gain, shift, scale = gain_ref[...], sh_ref[0], sc_ref[0]
    h = _rms_modulate(x, gain, shift, scale)
    h_prev = _rms_modulate(halo_ref[0], gain, shift, scale)
    h_prev = jnp.where(t > 0, h_prev, 0.0)
    h_ext = jnp.concatenate([h_prev, h], axis=0)
    pos = t * rows + lax.broadcasted_iota(jnp.int32, (rows, 1), 0)
    ys = []
    for g, window in enumerate(POOL_WINDOWS):
        cols = slice(g * group, (g + 1) * group)
        s = h_ext[:, cols]
        step = 1
        while step < window:
            s = s + pltpu.roll(s, step, axis=0)
            step *= 2
        inv_cnt = 1.0 / jnp.minimum(pos + 1, window).astype(F32)
        pooled = s[POOL_HALO:] * inv_cnt - h[:, cols]
        ys.append(_dot(pooled.astype(BF16), w_ref[g]))
    y = jnp.concatenate(ys, axis=-1) * ps_ref[...]
    o_ref[0] = x + gate_ref[0] * y


def _pool_layer(x, gain, shift, scale, gate, w_pool, pool_scale):
    b, s, d = x.shape
    groups, group, _ = w_pool.shape
    halo_blocks = POOL_ROWS // POOL_HALO
    vec = pl.BlockSpec((1, 1, d), lambda i, t: (i, 0, 0))
    row = pl.BlockSpec((1, d), lambda i, t: (0, 0))
    return pl.pallas_call(
        _pool_kernel,
        grid=(b, s // POOL_ROWS),
        in_specs=[
            pl.BlockSpec((1, POOL_ROWS, d), lambda i, t: (i, t, 0)),
            pl.BlockSpec((1, POOL_HALO, d),
                         lambda i, t: (i, jnp.maximum(t * halo_blocks - 1, 0), 0)),
            row, vec, vec, vec,
            pl.BlockSpec((groups, group, group), lambda i, t: (0, 0, 0)),
            row,
        ],
        out_specs=pl.BlockSpec((1, POOL_ROWS, d), lambda i, t: (i, t, 0)),
        out_shape=jax.ShapeDtypeStruct((b, s, d), F32),
        compiler_params=_params(("parallel", "parallel")),
        name="pool_mixer",
    )(x, x, gain.reshape(1, d), shift.reshape(b, 1, d), scale.reshape(b, 1, d),
      gate.reshape(b, 1, d), w_pool.astype(BF16), pool_scale.reshape(1, d))


def _mlp_kernel(x_ref, gain_ref, sh_ref, sc_ref, gate_ref, wup_ref, wdn_ref, fin_ref,
                o_ref, h_scr, acc_scr, *, final_norm):
    x = x_ref[0]
    h_scr[...] = _rms_modulate(x, gain_ref[...], sh_ref[0], sc_ref[0]).astype(BF16)
    acc_scr[...] = jnp.zeros_like(acc_scr)
    n_chunks = wup_ref.shape[1] // MLP_FF_CHUNK

    def chunk(f, carry):
        f0 = pl.multiple_of(f * MLP_FF_CHUNK, MLP_FF_CHUNK)
        up = _dot(h_scr[...], wup_ref[:, pl.ds(f0, MLP_FF_CHUNK)])
        a = jnp.maximum(up, 0.0)
        acc_scr[...] += _dot((a * a).astype(BF16), wdn_ref[pl.ds(f0, MLP_FF_CHUNK), :])
        return carry

    lax.fori_loop(0, n_chunks, chunk, 0)
    out = x + gate_ref[0] * acc_scr[...]
    if final_norm:
        out = out * lax.rsqrt(jnp.mean(out * out, axis=-1, keepdims=True) + EPS)
        out = out * fin_ref[...]
    o_ref[0] = out


def _mlp_layer(x, gain, shift, scale, gate, w_up, w_down, norm_final, final_norm):
    b, s, d = x.shape
    ff = w_up.shape[1]
    vec = pl.BlockSpec((1, 1, d), lambda i, t: (i, 0, 0))
    row = pl.BlockSpec((1, d), lambda i, t: (0, 0))
    resident = dict(pipeline_mode=pl.Buffered(1))
    return pl.pallas_call(
        functools.partial(_mlp_kernel, final_norm=final_norm),
        grid=(b, s // MLP_ROWS),
        in_specs=[
            pl.BlockSpec((1, MLP_ROWS, d), lambda i, t: (i, t, 0)),
            row, vec, vec, vec,
            pl.BlockSpec((d, ff), lambda i, t: (0, 0), **resident),
            pl.BlockSpec((ff, d), lambda i, t: (0, 0), **resident),
            row,
        ],
        out_specs=pl.BlockSpec((1, MLP_ROWS, d), lambda i, t: (i, t, 0)),
        out_shape=jax.ShapeDtypeStruct((b, s, d), F32),
        scratch_shapes=[pltpu.VMEM((MLP_ROWS, d), BF16), pltpu.VMEM((MLP_ROWS, d), F32)],
        compiler_params=_params(("parallel", "parallel")),
        name="mlp_final" if final_norm else "mlp",
    )(x, gain.reshape(1, d), shift.reshape(b, 1, d), scale.reshape(b, 1, d),
      gate.reshape(b, 1, d), w_up.astype(BF16), w_down.astype(BF16), norm_final.reshape(1, d))


def _qkv_kernel(x_ref, gain_ref, sh_ref, sc_ref, wqk_ref, wvt_ref,
                q_ref, k_ref, vt_ref, km_ref):
    t = pl.program_id(1)
    d = x_ref.shape[2]
    rows = x_ref.shape[1]
    blocks = rows // MOBA_BLOCK
    h = _rms_modulate(x_ref[0], gain_ref[...], sh_ref[0], sc_ref[0]).astype(BF16)
    q_ref[0] = (_dot(h, wqk_ref[:, :d]) * (HEAD_DIM ** -0.5)).astype(BF16)
    k = _dot(h, wqk_ref[:, d:])
    k_ref[0] = k.astype(BF16)
    k_mean = jnp.mean(k.reshape(blocks, MOBA_BLOCK, d), axis=1)
    steps = km_ref.shape[1] // blocks
    for step in range(steps):
        @pl.when(t % steps == step)
        def _():
            km_ref[0, step * blocks:(step + 1) * blocks, :] = k_mean
    vt = lax.dot_general(wvt_ref[...], h, (((1,), (1,)), ((), ())), preferred_element_type=F32)
    vt_ref[0] = vt.astype(BF16)


def _qkv_layer(x, gain, shift, scale, w_qkv):
    b, s, d = x.shape
    nb = s // MOBA_BLOCK
    km_rows = V7X_SUBLANES
    steps_per_km = km_rows * MOBA_BLOCK // QKV_ROWS
    vec = pl.BlockSpec((1, 1, d), lambda i, t: (i, 0, 0))
    row = pl.BlockSpec((1, d), lambda i, t: (0, 0))
    resident = dict(pipeline_mode=pl.Buffered(1))
    w_qk = w_qkv[:, :2 * d].astype(BF16)
    w_vt = w_qkv[:, 2 * d:].T.astype(BF16)
    return pl.pallas_call(
        _qkv_kernel,
        grid=(b, s // QKV_ROWS),
        in_specs=[
            pl.BlockSpec((1, QKV_ROWS, d), lambda i, t: (i, t, 0)),
            row, vec, vec,
            pl.BlockSpec((d, 2 * d), lambda i, t: (0, 0), **resident),
            pl.BlockSpec((d, d), lambda i, t: (0, 0), **resident),
        ],
        out_specs=[
            pl.BlockSpec((1, QKV_ROWS, d), lambda i, t: (i, t, 0)),
            pl.BlockSpec((1, QKV_ROWS, d), lambda i, t: (i, t, 0)),
            pl.BlockSpec((1, d, QKV_ROWS), lambda i, t: (i, 0, t)),
            pl.BlockSpec((1, km_rows, d), lambda i, t: (i, t // steps_per_km, 0)),
        ],
        out_shape=[
            jax.ShapeDtypeStruct((b, s, d), BF16),
            jax.ShapeDtypeStruct((b, s, d), BF16),
            jax.ShapeDtypeStruct((b, d, s), BF16),
            jax.ShapeDtypeStruct((b, nb, d), F32),
        ],
        compiler_params=_params(("parallel", "arbitrary")),
        name="qkv_proj",
    )(x, gain.reshape(1, d), shift.reshape(b, 1, d), scale.reshape(b, 1, d), w_qk, w_vt)


def _bias_kernel(rb_ref, o_ref):
    head = pl.program_id(0)
    kk = lax.broadcasted_iota(jnp.int32, (MOBA_BLOCK, MOBA_BLOCK), 0)
    qq = lax.broadcasted_iota(jnp.int32, (MOBA_BLOCK, MOBA_BLOCK), 1)
    max_exact = NUM_BUCKETS // 2
    for off in range(NEAR_OFFSETS):
        dist = MOBA_BLOCK * off + qq - kk
        dpos = jnp.maximum(dist, 0)
        nf = jnp.maximum(dpos, 1).astype(F32)
        large = max_exact + (jnp.log(nf / max_exact) / math.log(MAX_DISTANCE / max_exact)
                             * (NUM_BUCKETS - max_exact)).astype(jnp.int32)
        large = jnp.minimum(large, NUM_BUCKETS - 1)
        bucket = jnp.where(dpos < max_exact, dpos, large)
        val = jnp.zeros((MOBA_BLOCK, MOBA_BLOCK), F32)
        for bkt in range(NUM_BUCKETS):
            val = jnp.where(bucket == bkt, rb_ref[bkt, head], val)
        if off == 0:
            val = jnp.where(dist >= 0, val, -jnp.inf)
        o_ref[0, off] = val


def _bias_tiles(rel_bias):
    heads = rel_bias.shape[1]
    return pl.pallas_call(
        _bias_kernel,
        grid=(heads,),
        in_specs=[pl.BlockSpec(memory_space=pltpu.SMEM)],
        out_specs=pl.BlockSpec((1, NEAR_OFFSETS, MOBA_BLOCK, MOBA_BLOCK),
                               lambda h: (h, 0, 0, 0)),
        out_shape=jax.ShapeDtypeStruct((heads, NEAR_OFFSETS, MOBA_BLOCK, MOBA_BLOCK), F32),
        compiler_params=_params(("parallel",)),
        name="t5_bias_tiles",
    )(rel_bias)


def _attn_kernel(far_ref, q_ref, k_ref, vt_ref, km_ref, bias_ref, o_ref, mask_scr):
    hp = pl.program_id(1)
    i = pl.program_id(2)
    nb = km_ref.shape[1]
    neg = -jnp.inf
    q_t = q_ref[0].astype(F32).T
    dim_row = lax.broadcasted_iota(jnp.int32, q_t.shape, 0)
    blk = lax.broadcasted_iota(jnp.int32, (nb, MOBA_BLOCK), 0)
    blk_f = blk.astype(F32)
    k_mean = km_ref[0].astype(BF16)

    def key_block(j):
        return k_ref[0, pl.ds(pl.multiple_of(j * MOBA_BLOCK, MOBA_BLOCK), MOBA_BLOCK), :]

    def value_block(hh, j):
        return vt_ref[0, hh * HEAD_DIM:(hh + 1) * HEAD_DIM,
                      pl.ds(pl.multiple_of(j * MOBA_BLOCK, MOBA_BLOCK), MOBA_BLOCK)]

    def update(state, s, v_t):
        m, l, acc = state
        m_new = jnp.maximum(m, jnp.max(s, axis=0, keepdims=True))
        alpha = jnp.exp(m - m_new)
        p = jnp.exp(s - m_new)
        l = alpha * l + jnp.sum(p, axis=0, keepdims=True)
        acc = alpha * acc + _dot(v_t, p.astype(BF16))
        return m_new, l, acc

    q_heads, states = [], []
    for hh in range(HEADS_PER_STEP):
        own = (dim_row >= hh * HEAD_DIM) & (dim_row < (hh + 1) * HEAD_DIM)
        q_h = jnp.where(own, q_t, 0.0).astype(BF16)
        q_heads.append(q_h)
        gate = jnp.where(blk < i, _dot(k_mean, q_h), neg)
        chosen = jnp.zeros((nb, MOBA_BLOCK), F32)
        for _ in range(MOBA_TOPK):
            best = jnp.max(gate, axis=0, keepdims=True)
            first = jnp.min(jnp.where(gate == best, blk_f, float(nb)), axis=0, keepdims=True)
            hit = blk_f == first
            chosen = jnp.where(hit & (best > neg), 1.0, chosen)
            gate = jnp.where(hit, neg, gate)
        far_bias = far_ref[hp * HEADS_PER_STEP + hh]
        mask_scr[hh, 0] = jnp.where(chosen > 0.0, far_bias, neg)
        mask_scr[hh, 1] = jnp.where(chosen > 0.0, 0.0, neg)
        s = _dot(key_block(i), q_h) + bias_ref[hh, 0]
        m = jnp.max(s, axis=0, keepdims=True)
        p = jnp.exp(s - m)
        state = (m, jnp.sum(p, axis=0, keepdims=True), _dot(value_block(hh, i), p.astype(BF16)))
        for off in range(1, NEAR_OFFSETS):
            j = jnp.maximum(i - off, 0)
            sel_row = jnp.where(i >= off, mask_scr[hh, 1, pl.ds(j, 1), :], neg)
            s = _dot(key_block(j), q_h) + bias_ref[hh, off] + sel_row
            state = update(state, s, value_block(hh, j))
        states.append(state)

    def far_block(j, carry):
        k_j = key_block(j)
        out = []
        for hh in range(HEADS_PER_STEP):
            s = _dot(k_j, q_heads[hh]) + mask_scr[hh, 0, pl.ds(j, 1), :]
            out.append(update(carry[hh], s, value_block(hh, j)))
        return tuple(out)

    states = lax.fori_loop(0, jnp.maximum(i - (NEAR_OFFSETS - 1), 0), far_block, tuple(states))
    out_t = jnp.concatenate([acc * (1.0 / l) for (_, l, acc) in states], axis=0)
    o_ref[0] = out_t.T.astype(BF16)


def _attention(q, k, v_t, k_mean, bias_tiles, far_bias):
    b, s, d = q.shape
    nb = s // MOBA_BLOCK
    head_pairs = d // V7X_LANES
    return pl.pallas_call(
        _attn_kernel,
        grid=(b, head_pairs, nb),
        in_specs=[
            pl.BlockSpec(memory_space=pltpu.SMEM),
            pl.BlockSpec((1, MOBA_BLOCK, V7X_LANES), lambda bi, hp, i: (bi, i, hp)),
            pl.BlockSpec((1, s, V7X_LANES), lambda bi, hp, i: (bi, 0, hp)),
            pl.BlockSpec((1, V7X_LANES, s), lambda bi, hp, i: (bi, hp, 0)),
            pl.BlockSpec((1, nb, V7X_LANES), lambda bi, hp, i: (bi, 0, hp)),
            pl.BlockSpec((HEADS_PER_STEP, NEAR_OFFSETS, MOBA_BLOCK, MOBA_BLOCK),
                         lambda bi, hp, i: (hp, 0, 0, 0)),
        ],
        out_specs=pl.BlockSpec((1, MOBA_BLOCK, V7X_LANES), lambda bi, hp, i: (bi, i, hp)),
        out_shape=jax.ShapeDtypeStruct((b, s, d), BF16),
        scratch_shapes=[pltpu.VMEM((HEADS_PER_STEP, 2, nb, MOBA_BLOCK), F32)],
        compiler_params=_params(("parallel", "parallel", "arbitrary")),
        name="moba_attention",
    )(far_bias, q, k, v_t, k_mean, bias_tiles)


def _oproj_kernel(x_ref, o_ref_in, w_ref, gate_ref, out_ref):
    out_ref[0] = x_ref[0] + gate_ref[0] * _dot(o_ref_in[0], w_ref[...])


def _oproj_layer(x, o, w_o, gate):
    b, s, d = x.shape
    tile = pl.BlockSpec((1, MLP_ROWS, d), lambda i, t: (i, t, 0))
    return pl.pallas_call(
        _oproj_kernel,
        grid=(b, s // MLP_ROWS),
        in_specs=[
            tile, tile,
            pl.BlockSpec((d, d), lambda i, t: (0, 0), pipeline_mode=pl.Buffered(1)),
            pl.BlockSpec((1, 1, d), lambda i, t: (i, 0, 0)),
        ],
        out_specs=tile,
        out_shape=jax.ShapeDtypeStruct((b, s, d), F32),
        compiler_params=_params(("parallel", "parallel")),
        name="attn_out_proj",
    )(x, o, w_o.astype(BF16), gate.reshape(b, 1, d))


def kernel(x, c, rel_bias, w_mod, b_mod, norm_mix, norm_mlp, w_pool, pool_scale,
           w_qkv, w_o, w_up, w_down, norm_final):
    b, s, d = x.shape
    assert d % V7X_LANES == 0 and s % MLP_ROWS == 0 and s % (V7X_SUBLANES * MOBA_BLOCK) == 0
    assert w_mod.shape[0] == 2 and w_mod.shape[2] % MOD_COLS == 0
    mod = _modulation(c, w_mod, b_mod)
    sh1, sc1, g1, sh2, sc2, g2 = (mod[:, :, n * d:(n + 1) * d] for n in range(6))

    x = _pool_layer(x, norm_mix[0], sh1[0], sc1[0], g1[0], w_pool[0], pool_scale[0])
    x = _mlp_layer(x, norm_mlp[0], sh2[0], sc2[0], g2[0], w_up[0], w_down[0], norm_final, False)

    q, k, v_t, k_mean = _qkv_layer(x, norm_mix[1], sh1[1], sc1[1], w_qkv[0])
    o = _attention(q, k, v_t, k_mean, _bias_tiles(rel_bias), rel_bias[NUM_BUCKETS - 1])
    x = _oproj_layer(x, o, w_o[0], g1[1])
    return _mlp_layer(x, norm_mlp[1], sh2[1], sc2[1], g2[1], w_up[1], w_down[1], norm_final, True)
```

```python
import functools
import math

import jax
import jax.numpy as jnp
from jax import lax
from jax.experimental import pallas as pl
from jax.experimental.pallas import tpu as pltpu

F32 = jnp.float32
BF16 = jnp.bfloat16

HEAD_DIM = 64
MOBA_BLOCK = 256
MOBA_TOPK = 3
NUM_BUCKETS = 32
MAX_DISTANCE = 1024
POOL_WINDOWS = (2, 4, 8, 16)
EPS = 1e-6

V7X_LANES = 128
V7X_SUBLANES = 8
V7X_VMEM_LIMIT_BYTES = 56 * 1024 * 1024

NEAR_OFFSETS = 5
BIAS_TILES = NEAR_OFFSETS + 1
KEY_GROUP = 4
HEADS_PER_STEP = V7X_LANES // HEAD_DIM
POOL_HALO = max(POOL_WINDOWS)

POOL_ROWS = 512
MLP_ROWS = 512
MLP_FF_CHUNK = 512
QKV_ROWS = 512
MOD_COLS = 1536


def _params(semantics):
    return pltpu.CompilerParams(dimension_semantics=semantics,
                                vmem_limit_bytes=V7X_VMEM_LIMIT_BYTES)


def _dot(a, b):
    return jnp.dot(a, b, preferred_element_type=F32)


def _rms_modulate(x, gain, shift, scale):
    y = x * lax.rsqrt(jnp.mean(x * x, axis=-1, keepdims=True) + EPS)
    return (y * gain) * (1.0 + scale) + shift


def _mod_kernel(c_ref, w_ref, b_ref, o_ref):
    c = c_ref[...]
    a = c * (1.0 / (1.0 + jnp.exp(-c)))
    w = w_ref[0]
    a_hi = a.astype(BF16)
    a_lo = (a - a_hi.astype(F32)).astype(BF16)
    w_hi = w.astype(BF16)
    w_lo = (w - w_hi.astype(F32)).astype(BF16)
    acc = _dot(a_hi, w_hi) + _dot(a_hi, w_lo) + _dot(a_lo, w_hi)
    o_ref[0] = acc + b_ref[0]


def _modulation(c, w_mod, b_mod):
    depth, d, n = w_mod.shape
    b = c.shape[0]
    rows = -(-b // V7X_SUBLANES) * V7X_SUBLANES
    c_pad = jnp.pad(c, ((0, rows - b), (0, 0)))
    out = pl.pallas_call(
        _mod_kernel,
        grid=(depth, n // MOD_COLS),
        in_specs=[
            pl.BlockSpec((rows, d), lambda i, j: (0, 0)),
            pl.BlockSpec((1, d, MOD_COLS), lambda i, j: (i, 0, j)),
            pl.BlockSpec((1, 1, MOD_COLS), lambda i, j: (i, 0, j)),
        ],
        out_specs=pl.BlockSpec((1, rows, MOD_COLS), lambda i, j: (i, 0, j)),
        out_shape=jax.ShapeDtypeStruct((depth, rows, n), F32),
        compiler_params=_params(("parallel", "parallel")),
        name="modulation",
    )(c_pad, w_mod, b_mod.reshape(depth, 1, n))
    return out[:, :b]


def _pool_kernel(x_ref, halo_ref, gain_ref, sh_ref, sc_ref, gate_ref, w_ref, ps_ref, o_ref):
    t = pl.program_id(1)
    rows = x_ref.shape[1]
    group = w_ref.shape[1]
    x = x_ref[0]
    gain, shift, scale = gain_ref[...], sh_ref[0], sc_ref[0]
    h = _rms_modulate(x, gain, shift, scale)
    h_prev = _rms_modulate(halo_ref[0], gain, shift, scale)
    h_prev = jnp.where(t > 0, h_prev, 0.0)
    h_ext = jnp.concatenate([h_prev, h], axis=0)
    pos = t * rows + lax.broadcasted_iota(jnp.int32, (rows, 1), 0)
    ys = []
    for g, window in enumerate(POOL_WINDOWS):
        cols = slice(g * group, (g + 1) * group)
        s = h_ext[:, cols]
        step = 1
        while step < window:
            s = s + pltpu.roll(s, step, axis=0)
            step *= 2
        inv_cnt = 1.0 / jnp.minimum(pos + 1, window).astype(F32)
        pooled = s[POOL_HALO:] * inv_cnt - h[:, cols]
        ys.append(_dot(pooled.astype(BF16), w_ref[g]))
    y = jnp.concatenate(ys, axis=-1) * ps_ref[...]
    o_ref[0] = x + gate_ref[0] * y


def _pool_layer(x, gain, shift, scale, gate, w_pool, pool_scale):
    b, s, d = x.shape
    groups, group, _ = w_pool.shape
    halo_blocks = POOL_ROWS // POOL_HALO
    vec = pl.BlockSpec((1, 1, d), lambda i, t: (i, 0, 0))
    row = pl.BlockSpec((1, d), lambda i, t: (0, 0))
    return pl.pallas_call(
        _pool_kernel,
        grid=(b, s // POOL_ROWS),
        in_specs=[
            pl.BlockSpec((1, POOL_ROWS, d), lambda i, t: (i, t, 0)),
            pl.BlockSpec((1, POOL_HALO, d),
                         lambda i, t: (i, jnp.maximum(t * halo_blocks - 1, 0), 0)),
            row, vec, vec, vec,
            pl.BlockSpec((groups, group, group), lambda i, t: (0, 0, 0)),
            row,
        ],
        out_specs=pl.BlockSpec((1, POOL_ROWS, d), lambda i, t: (i, t, 0)),
        out_shape=jax.ShapeDtypeStruct((b, s, d), F32),
        compiler_params=_params(("parallel", "parallel")),
        name="pool_mixer",
    )(x, x, gain.reshape(1, d), shift.reshape(b, 1, d), scale.reshape(b, 1, d),
      gate.reshape(b, 1, d), w_pool.astype(BF16), pool_scale.reshape(1, d))


def _mlp_kernel(x_ref, gain_ref, sh_ref, sc_ref, gate_ref, wup_ref, wdn_ref, fin_ref,
                o_ref, h_scr, acc_scr, *, final_norm):
    x = x_ref[0]
    h_scr[...] = _rms_modulate(x, gain_ref[...], sh_ref[0], sc_ref[0]).astype(BF16)
    acc_scr[...] = jnp.zeros_like(acc_scr)
    n_chunks = wup_ref.shape[1] // MLP_FF_CHUNK

    def chunk(f, carry):
        f0 = pl.multiple_of(f * MLP_FF_CHUNK, MLP_FF_CHUNK)
        up = _dot(h_scr[...], wup_ref[:, pl.ds(f0, MLP_FF_CHUNK)])
        a = jnp.maximum(up, 0.0)
        acc_scr[...] += _dot((a * a).astype(BF16), wdn_ref[pl.ds(f0, MLP_FF_CHUNK), :])
        return carry

    lax.fori_loop(0, n_chunks, chunk, 0)
    out = x + gate_ref[0] * acc_scr[...]
    if final_norm:
        out = out * lax.rsqrt(jnp.mean(out * out, axis=-1, keepdims=True) + EPS)
        out = out * fin_ref[...]
    o_ref[0] = out


def _mlp_layer(x, gain, shift, scale, gate, w_up, w_down, norm_final, final_norm):
    b, s, d = x.shape
    ff = w_up.shape[1]
    vec = pl.BlockSpec((1, 1, d), lambda i, t: (i, 0, 0))
    row = pl.BlockSpec((1, d), lambda i, t: (0, 0))
    resident = dict(pipeline_mode=pl.Buffered(1))
    return pl.pallas_call(
        functools.partial(_mlp_kernel, final_norm=final_norm),
        grid=(b, s // MLP_ROWS),
        in_specs=[
            pl.BlockSpec((1, MLP_ROWS, d), lambda i, t: (i, t, 0)),
            row, vec, vec, vec,
            pl.BlockSpec((d, ff), lambda i, t: (0, 0), **resident),
            pl.BlockSpec((ff, d), lambda i, t: (0, 0), **resident),
            row,
        ],
        out_specs=pl.BlockSpec((1, MLP_ROWS, d), lambda i, t: (i, t, 0)),
        out_shape=jax.ShapeDtypeStruct((b, s, d), F32),
        scratch_shapes=[pltpu.VMEM((MLP_ROWS, d), BF16), pltpu.VMEM((MLP_ROWS, d), F32)],
        compiler_params=_params(("parallel", "parallel")),
        name="mlp_final" if final_norm else "mlp",
    )(x, gain.reshape(1, d), shift.reshape(b, 1, d), scale.reshape(b, 1, d),
      gate.reshape(b, 1, d), w_up.astype(BF16), w_down.astype(BF16), norm_final.reshape(1, d))


def _qkv_kernel(x_ref, gain_ref, sh_ref, sc_ref, wqk_ref, wvt_ref,
                q_ref, k_ref, vt_ref, km_ref):
    t = pl.program_id(1)
    d = x_ref.shape[2]
    rows = x_ref.shape[1]
    blocks = rows // MOBA_BLOCK
    h = _rms_modulate(x_ref[0], gain_ref[...], sh_ref[0], sc_ref[0]).astype(BF16)
    q_ref[0] = (_dot(h, wqk_ref[:, :d]) * (HEAD_DIM ** -0.5)).astype(BF16)
    k = _dot(h, wqk_ref[:, d:])
    k_ref[0] = k.astype(BF16)
    k_mean = jnp.mean(k.reshape(blocks, MOBA_BLOCK, d), axis=1)
    steps = km_ref.shape[1] // blocks
    for step in range(steps):
        @pl.when(t % steps == step)
        def _():
            km_ref[0, step * blocks:(step + 1) * blocks, :] = k_mean
    vt = lax.dot_general(wvt_ref[...], h, (((1,), (1,)), ((), ())), preferred_element_type=F32)
    vt_ref[0] = vt.astype(BF16)


def _qkv_layer(x, gain, shift, scale, w_qkv):
    b, s, d = x.shape
    nb = s // MOBA_BLOCK
    km_rows = V7X_SUBLANES
    steps_per_km = km_rows * MOBA_BLOCK // QKV_ROWS
    vec = pl.BlockSpec((1, 1, d), lambda i, t: (i, 0, 0))
    row = pl.BlockSpec((1, d), lambda i, t: (0, 0))
    resident = dict(pipeline_mode=pl.Buffered(1))
    w_qk = w_qkv[:, :2 * d].astype(BF16)
    w_vt = w_qkv[:, 2 * d:].T.astype(BF16)
    return pl.pallas_call(
        _qkv_kernel,
        grid=(b, s // QKV_ROWS),
        in_specs=[
            pl.BlockSpec((1, QKV_ROWS, d), lambda i, t: (i, t, 0)),
            row, vec, vec,
            pl.BlockSpec((d, 2 * d), lambda i, t: (0, 0), **resident),
            pl.BlockSpec((d, d), lambda i, t: (0, 0), **resident),
        ],
        out_specs=[
            pl.BlockSpec((1, QKV_ROWS, d), lambda i, t: (i, t, 0)),
            pl.BlockSpec((1, QKV_ROWS, d), lambda i, t: (i, t, 0)),
            pl.BlockSpec((1, d, QKV_ROWS), lambda i, t: (i, 0, t)),
            pl.BlockSpec((1, km_rows, d), lambda i, t: (i, t // steps_per_km, 0)),
        ],
        out_shape=[
            jax.ShapeDtypeStruct((b, s, d), BF16),
            jax.ShapeDtypeStruct((b, s, d), BF16),
            jax.ShapeDtypeStruct((b, d, s), BF16),
            jax.ShapeDtypeStruct((b, nb, d), F32),
        ],
        compiler_params=_params(("parallel", "arbitrary")),
        name="qkv_proj",
    )(x, gain.reshape(1, d), shift.reshape(b, 1, d), scale.reshape(b, 1, d), w_qk, w_vt)


def _bias_kernel(rb_ref, o_ref):
    head = pl.program_id(0)
    kk = lax.broadcasted_iota(jnp.int32, (MOBA_BLOCK, MOBA_BLOCK), 0)
    qq = lax.broadcasted_iota(jnp.int32, (MOBA_BLOCK, MOBA_BLOCK), 1)
    max_exact = NUM_BUCKETS // 2
    for off in range(BIAS_TILES):
        dist = MOBA_BLOCK * off + qq - kk
        dpos = jnp.maximum(dist, 0)
        nf = jnp.maximum(dpos, 1).astype(F32)
        large = max_exact + (jnp.log(nf / max_exact) / math.log(MAX_DISTANCE / max_exact)
                             * (NUM_BUCKETS - max_exact)).astype(jnp.int32)
        large = jnp.minimum(large, NUM_BUCKETS - 1)
        bucket = jnp.where(dpos < max_exact, dpos, large)
        val = jnp.zeros((MOBA_BLOCK, MOBA_BLOCK), F32)
        for bkt in range(NUM_BUCKETS):
            val = jnp.where(bucket == bkt, rb_ref[bkt, head], val)
        if off == 0:
            val = jnp.where(dist >= 0, val, -jnp.inf)
        o_ref[0, off] = val


def _bias_tiles(rel_bias):
    heads = rel_bias.shape[1]
    return pl.pallas_call(
        _bias_kernel,
        grid=(heads,),
        in_specs=[pl.BlockSpec(memory_space=pltpu.SMEM)],
        out_specs=pl.BlockSpec((1, BIAS_TILES, MOBA_BLOCK, MOBA_BLOCK),
                               lambda h: (h, 0, 0, 0)),
        out_shape=jax.ShapeDtypeStruct((heads, BIAS_TILES, MOBA_BLOCK, MOBA_BLOCK), F32),
        compiler_params=_params(("parallel",)),
        name="t5_bias_tiles",
    )(rel_bias)


def _attn_kernel(far_ref, q_ref, k_ref, vt_ref, km_ref, bias_ref, o_ref, mask_scr):
    hp = pl.program_id(1)
    i = pl.program_id(2)
    nb = km_ref.shape[1]
    neg = -jnp.inf
    q_t = q_ref[0].astype(F32).T
    dim_row = lax.broadcasted_iota(jnp.int32, q_t.shape, 0)
    blk = lax.broadcasted_iota(jnp.int32, (nb, MOBA_BLOCK), 0)
    blk_f = blk.astype(F32)
    k_mean = km_ref[0].astype(BF16)
    own_group = i // KEY_GROUP

    def key_block(j):
        return k_ref[0, pl.ds(pl.multiple_of(j * MOBA_BLOCK, MOBA_BLOCK), MOBA_BLOCK), :]

    def value_block(hh, j):
        return vt_ref[0, hh * HEAD_DIM:(hh + 1) * HEAD_DIM,
                      pl.ds(pl.multiple_of(j * MOBA_BLOCK, MOBA_BLOCK), MOBA_BLOCK)]

    def update(state, scores, hh, first_block):
        m, l, acc = state
        m_new = m
        for s in scores:
            m_new = jnp.maximum(m_new, jnp.max(s, axis=0, keepdims=True))
        alpha = jnp.exp(m - m_new)
        l = alpha * l
        acc = alpha * acc
        for n, s in enumerate(scores):
            p = jnp.exp(s - m_new)
            l = l + jnp.sum(p, axis=0, keepdims=True)
            acc = acc + _dot(value_block(hh, first_block + n), p.astype(BF16))
        return m_new, l, acc

    def near_group(state, hh, q_h, group, valid):
        first_block = group * KEY_GROUP
        scores = []
        for n in range(KEY_GROUP):
            j = first_block + n
            off = i - j
            sel_row = jnp.where(off > 0, mask_scr[hh, 1, pl.ds(j, 1), :], neg)
            sel_row = jnp.where(off == 0, 0.0, sel_row)
            sel_row = jnp.where(valid, sel_row, neg)
            tile = bias_ref[hh, jnp.clip(off, 0, NEAR_OFFSETS)]
            scores.append(_dot(key_block(j), q_h) + tile + sel_row)
        return update(state, scores, hh, first_block)

    q_heads, states = [], []
    for hh in range(HEADS_PER_STEP):
        own = (dim_row >= hh * HEAD_DIM) & (dim_row < (hh + 1) * HEAD_DIM)
        q_h = jnp.where(own, q_t, 0.0).astype(BF16)
        q_heads.append(q_h)
        gate = jnp.where(blk < i, _dot(k_mean, q_h), neg)
        chosen = jnp.zeros((nb, MOBA_BLOCK), F32)
        for _ in range(MOBA_TOPK):
            best = jnp.max(gate, axis=0, keepdims=True)
            first = jnp.min(jnp.where(gate == best, blk_f, float(nb)), axis=0, keepdims=True)
            hit = blk_f == first
            chosen = jnp.where(hit & (best > neg), 1.0, chosen)
            gate = jnp.where(hit, neg, gate)
        far_bias = far_ref[hp * HEADS_PER_STEP + hh]
        mask_scr[hh, 0] = jnp.where(chosen > 0.0, far_bias, neg)
        mask_scr[hh, 1] = jnp.where(chosen > 0.0, 0.0, neg)
        state = (jnp.full((1, MOBA_BLOCK), neg, F32), jnp.zeros((1, MOBA_BLOCK), F32),
                 jnp.zeros((HEAD_DIM, MOBA_BLOCK), F32))
        state = near_group(state, hh, q_h, own_group, True)
        state = near_group(state, hh, q_h, jnp.maximum(own_group - 1, 0), own_group > 0)
        states.append(state)

    def far_group(group, carry):
        first_block = group * KEY_GROUP
        scores = [[_dot(key_block(first_block + n), q_heads[hh])
                   + mask_scr[hh, 0, pl.ds(first_block + n, 1), :] for n in range(KEY_GROUP)]
                  for hh in range(HEADS_PER_STEP)]
        return tuple(update(carry[hh], scores[hh], hh, first_block)
                     for hh in range(HEADS_PER_STEP))

    states = lax.fori_loop(0, jnp.maximum(own_group - 1, 0), far_group, tuple(states))
    out_t = jnp.concatenate([acc * (1.0 / l) for (_, l, acc) in states], axis=0)
    o_ref[0] = out_t.T.astype(BF16)


def _attention(q, k, v_t, k_mean, bias_tiles, far_bias):
    b, s, d = q.shape
    nb = s // MOBA_BLOCK
    head_pairs = d // V7X_LANES
    return pl.pallas_call(
        _attn_kernel,
        grid=(b, head_pairs, nb),
        in_specs=[
            pl.BlockSpec(memory_space=pltpu.SMEM),
            pl.BlockSpec((1, MOBA_BLOCK, V7X_LANES), lambda bi, hp, i: (bi, i, hp)),
            pl.BlockSpec((1, s, V7X_LANES), lambda bi, hp, i: (bi, 0, hp)),
            pl.BlockSpec((1, V7X_LANES, s), lambda bi, hp, i: (bi, hp, 0)),
            pl.BlockSpec((1, nb, V7X_LANES), lambda bi, hp, i: (bi, 0, hp)),
            pl.BlockSpec((HEADS_PER_STEP, BIAS_TILES, MOBA_BLOCK, MOBA_BLOCK),
                         lambda bi, hp, i: (hp, 0, 0, 0)),
        ],
        out_specs=pl.BlockSpec((1, MOBA_BLOCK, V7X_LANES), lambda bi, hp, i: (bi, i, hp)),
        out_shape=jax.ShapeDtypeStruct((b, s, d), BF16),
        scratch_shapes=[pltpu.VMEM((HEADS_PER_STEP, 2, nb, MOBA_BLOCK), F32)],
        compiler_params=_params(("parallel", "parallel", "arbitrary")),
        name="moba_attention",
    )(far_bias, q, k, v_t, k_mean, bias_tiles)


def _oproj_kernel(x_ref, o_ref_in, w_ref, gate_ref, out_ref):
    out_ref[0] = x_ref[0] + gate_ref[0] * _dot(o_ref_in[0], w_ref[...])


def _oproj_layer(x, o, w_o, gate):
    b, s, d = x.shape
    tile = pl.BlockSpec((1, MLP_ROWS, d), lambda i, t: (i, t, 0))
    return pl.pallas_call(
        _oproj_kernel,
        grid=(b, s // MLP_ROWS),
        in_specs=[
            tile, tile,
            pl.BlockSpec((d, d), lambda i, t: (0, 0), pipeline_mode=pl.Buffered(1)),
            pl.BlockSpec((1, 1, d), lambda i, t: (i, 0, 0)),
        ],
        out_specs=tile,
        out_shape=jax.ShapeDtypeStruct((b, s, d), F32),
        compiler_params=_params(("parallel", "parallel")),
        name="attn_out_proj",
    )(x, o, w_o.astype(BF16), gate.reshape(b, 1, d))


def kernel(x, c, rel_bias, w_mod, b_mod, norm_mix, norm_mlp, w_pool, pool_scale,
           w_qkv, w_o, w_up, w_down, norm_final):
    b, s, d = x.shape
    assert d % V7X_LANES == 0 and s % MLP_ROWS == 0 and s % (V7X_SUBLANES * MOBA_BLOCK) == 0
    assert w_mod.shape[0] == 2 and w_mod.shape[2] % MOD_COLS == 0
    mod = _modulation(c, w_mod, b_mod)
    sh1, sc1, g1, sh2, sc2, g2 = (mod[:, :, n * d:(n + 1) * d] for n in range(6))

    x = _pool_layer(x, norm_mix[0], sh1[0], sc1[0], g1[0], w_pool[0], pool_scale[0])
    x = _mlp_layer(x, norm_mlp[0], sh2[0], sc2[0], g2[0], w_up[0], w_down[0], norm_final, False)

    q, k, v_t, k_mean = _qkv_layer(x, norm_mix[1], sh1[1], sc1[1], w_qkv[0])
    o = _attention(q, k, v_t, k_mean, _bias_tiles(rel_bias), rel_bias[NUM_BUCKETS - 1])
    x = _oproj_layer(x, o, w_o[0], g1[1])
    return _mlp_layer(x, norm_mlp[1], sh2[1], sc2[1], g2[1], w_up[1], w_down[1], norm_final, True)
```

```python
import functools
import math

import jax
import jax.numpy as jnp
from jax import lax
from jax.experimental import pallas as pl
from jax.experimental.pallas import tpu as pltpu

F32 = jnp.float32
BF16 = jnp.bfloat16

HEAD_DIM = 64
MOBA_BLOCK = 256
MOBA_TOPK = 3
NUM_BUCKETS = 32
MAX_DISTANCE = 1024
POOL_WINDOWS = (2, 4, 8, 16)
EPS = 1e-6

V7X_LANES = 128
V7X_SUBLANES = 8
V7X_VMEM_LIMIT_BYTES = 56 * 1024 * 1024

NEAR_OFFSETS = 5
BIAS_TILES = NEAR_OFFSETS + 1
KEY_GROUP = 4
DENOM_ROWS = 16
LOG2E = math.log2(math.e)
HEADS_PER_STEP = V7X_LANES // HEAD_DIM
POOL_HALO = max(POOL_WINDOWS)

POOL_ROWS = 512
MLP_ROWS = 512
MLP_FF_CHUNK = 512
QKV_ROWS = 512
MOD_COLS = 1536


def _params(semantics):
    return pltpu.CompilerParams(dimension_semantics=semantics,
                                vmem_limit_bytes=V7X_VMEM_LIMIT_BYTES)


def _dot(a, b):
    return jnp.dot(a, b, preferred_element_type=F32)


def _rms_modulate(x, gain, shift, scale):
    y = x * lax.rsqrt(jnp.mean(x * x, axis=-1, keepdims=True) + EPS)
    return (y * gain) * (1.0 + scale) + shift


def _mod_kernel(c_ref, w_ref, b_ref, o_ref):
    c = c_ref[...]
    a = c * (1.0 / (1.0 + jnp.exp(-c)))
    w = w_ref[0]
    a_hi = a.astype(BF16)
    a_lo = (a - a_hi.astype(F32)).astype(BF16)
    w_hi = w.astype(BF16)
    w_lo = (w - w_hi.astype(F32)).astype(BF16)
    acc = _dot(a_hi, w_hi) + _dot(a_hi, w_lo) + _dot(a_lo, w_hi)
    o_ref[0] = acc + b_ref[0]


def _modulation(c, w_mod, b_mod):
    depth, d, n = w_mod.shape
    b = c.shape[0]
    rows = -(-b // V7X_SUBLANES) * V7X_SUBLANES
    c_pad = jnp.pad(c, ((0, rows - b), (0, 0)))
    out = pl.pallas_call(
        _mod_kernel,
        grid=(depth, n // MOD_COLS),
        in_specs=[
            pl.BlockSpec((rows, d), lambda i, j: (0, 0)),
            pl.BlockSpec((1, d, MOD_COLS), lambda i, j: (i, 0, j)),
            pl.BlockSpec((1, 1, MOD_COLS), lambda i, j: (i, 0, j)),
        ],
        out_specs=pl.BlockSpec((1, rows, MOD_COLS), lambda i, j: (i, 0, j)),
        out_shape=jax.ShapeDtypeStruct((depth, rows, n), F32),
        compiler_params=_params(("parallel", "parallel")),
        name="modulation",
    )(c_pad, w_mod, b_mod.reshape(depth, 1, n))
    return out[:, :b]


def _pool_kernel(x_ref, halo_ref, gain_ref, sh_ref, sc_ref, gate_ref, w_ref, ps_ref, o_ref):
    t = pl.program_id(1)
    rows = x_ref.shape[1]
    group = w_ref.shape[1]
    x = x_ref[0]
    gain, shift, scale = gain_ref[...], sh_ref[0], sc_ref[0]
    h = _rms_modulate(x, gain, shift, scale)
    h_prev = _rms_modulate(halo_ref[0], gain, shift, scale)
    h_prev = jnp.where(t > 0, h_prev, 0.0)
    h_ext = jnp.concatenate([h_prev, h], axis=0)
    pos = t * rows + lax.broadcasted_iota(jnp.int32, (rows, 1), 0)
    ys = []
    for g, window in enumerate(POOL_WINDOWS):
        cols = slice(g * group, (g + 1) * group)
        s = h_ext[:, cols]
        step = 1
        while step < window:
            s = s + pltpu.roll(s, step, axis=0)
            step *= 2
        inv_cnt = 1.0 / jnp.minimum(pos + 1, window).astype(F32)
        pooled = s[POOL_HALO:] * inv_cnt - h[:, cols]
        ys.append(_dot(pooled.astype(BF16), w_ref[g]))
    y = jnp.concatenate(ys, axis=-1) * ps_ref[...]
    o_ref[0] = x + gate_ref[0] * y


def _pool_layer(x, gain, shift, scale, gate, w_pool, pool_scale):
    b, s, d = x.shape
    groups, group, _ = w_pool.shape
    halo_blocks = POOL_ROWS // POOL_HALO
    vec = pl.BlockSpec((1, 1, d), lambda i, t: (i, 0, 0))
    row = pl.BlockSpec((1, d), lambda i, t: (0, 0))
    return pl.pallas_call(
        _pool_kernel,
        grid=(b, s // POOL_ROWS),
        in_specs=[
            pl.BlockSpec((1, POOL_ROWS, d), lambda i, t: (i, t, 0)),
            pl.BlockSpec((1, POOL_HALO, d),
                         lambda i, t: (i, jnp.maximum(t * halo_blocks - 1, 0), 0)),
            row, vec, vec, vec,
            pl.BlockSpec((groups, group, group), lambda i, t: (0, 0, 0)),
            row,
        ],
        out_specs=pl.BlockSpec((1, POOL_ROWS, d), lambda i, t: (i, t, 0)),
        out_shape=jax.ShapeDtypeStruct((b, s, d), F32),
        compiler_params=_params(("parallel", "parallel")),
        name="pool_mixer",
    )(x, x, gain.reshape(1, d), shift.reshape(b, 1, d), scale.reshape(b, 1, d),
      gate.reshape(b, 1, d), w_pool.astype(BF16), pool_scale.reshape(1, d))


def _mlp_kernel(x_ref, gain_ref, sh_ref, sc_ref, gate_ref, wup_ref, wdn_ref, fin_ref,
                o_ref, h_scr, acc_scr, *, final_norm):
    x = x_ref[0]
    h_scr[...] = _rms_modulate(x, gain_ref[...], sh_ref[0], sc_ref[0]).astype(BF16)
    acc_scr[...] = jnp.zeros_like(acc_scr)
    n_chunks = wup_ref.shape[1] // MLP_FF_CHUNK

    def chunk(f, carry):
        f0 = pl.multiple_of(f * MLP_FF_CHUNK, MLP_FF_CHUNK)
        up = _dot(h_scr[...], wup_ref[:, pl.ds(f0, MLP_FF_CHUNK)])
        a = jnp.maximum(up, 0.0)
        acc_scr[...] += _dot((a * a).astype(BF16), wdn_ref[pl.ds(f0, MLP_FF_CHUNK), :])
        return carry

    lax.fori_loop(0, n_chunks, chunk, 0)
    out = x + gate_ref[0] * acc_scr[...]
    if final_norm:
        out = out * lax.rsqrt(jnp.mean(out * out, axis=-1, keepdims=True) + EPS)
        out = out * fin_ref[...]
    o_ref[0] = out


def _mlp_layer(x, gain, shift, scale, gate, w_up, w_down, norm_final, final_norm):
    b, s, d = x.shape
    ff = w_up.shape[1]
    vec = pl.BlockSpec((1, 1, d), lambda i, t: (i, 0, 0))
    row = pl.BlockSpec((1, d), lambda i, t: (0, 0))
    resident = dict(pipeline_mode=pl.Buffered(1))
    return pl.pallas_call(
        functools.partial(_mlp_kernel, final_norm=final_norm),
        grid=(b, s // MLP_ROWS),
        in_specs=[
            pl.BlockSpec((1, MLP_ROWS, d), lambda i, t: (i, t, 0)),
            row, vec, vec, vec,
            pl.BlockSpec((d, ff), lambda i, t: (0, 0), **resident),
            pl.BlockSpec((ff, d), lambda i, t: (0, 0), **resident),
            row,
        ],
        out_specs=pl.BlockSpec((1, MLP_ROWS, d), lambda i, t: (i, t, 0)),
        out_shape=jax.ShapeDtypeStruct((b, s, d), F32),
        scratch_shapes=[pltpu.VMEM((MLP_ROWS, d), BF16), pltpu.VMEM((MLP_ROWS, d), F32)],
        compiler_params=_params(("parallel", "parallel")),
        name="mlp_final" if final_norm else "mlp",
    )(x, gain.reshape(1, d), shift.reshape(b, 1, d), scale.reshape(b, 1, d),
      gate.reshape(b, 1, d), w_up.astype(BF16), w_down.astype(BF16), norm_final.reshape(1, d))


def _qkv_kernel(x_ref, gain_ref, sh_ref, sc_ref, wqk_ref, wvt_ref,
                q_ref, k_ref, vt_ref, km_ref):
    t = pl.program_id(1)
    d = x_ref.shape[2]
    rows = x_ref.shape[1]
    blocks = rows // MOBA_BLOCK
    h = _rms_modulate(x_ref[0], gain_ref[...], sh_ref[0], sc_ref[0]).astype(BF16)
    q_ref[0] = (_dot(h, wqk_ref[:, :d]) * (HEAD_DIM ** -0.5 * LOG2E)).astype(BF16)
    k = _dot(h, wqk_ref[:, d:])
    k_ref[0] = k.astype(BF16)
    k_mean = jnp.mean(k.reshape(blocks, MOBA_BLOCK, d), axis=1)
    steps = km_ref.shape[1] // blocks
    for step in range(steps):
        @pl.when(t % steps == step)
        def _():
            km_ref[0, step * blocks:(step + 1) * blocks, :] = k_mean
    vt = lax.dot_general(wvt_ref[...], h, (((1,), (1,)), ((), ())), preferred_element_type=F32)
    vt_ref[0] = vt.astype(BF16)


def _qkv_layer(x, gain, shift, scale, w_qkv):
    b, s, d = x.shape
    nb = s // MOBA_BLOCK
    km_rows = V7X_SUBLANES
    steps_per_km = km_rows * MOBA_BLOCK // QKV_ROWS
    vec = pl.BlockSpec((1, 1, d), lambda i, t: (i, 0, 0))
    row = pl.BlockSpec((1, d), lambda i, t: (0, 0))
    resident = dict(pipeline_mode=pl.Buffered(1))
    w_qk = w_qkv[:, :2 * d].astype(BF16)
    w_vt = w_qkv[:, 2 * d:].T.astype(BF16)
    return pl.pallas_call(
        _qkv_kernel,
        grid=(b, s // QKV_ROWS),
        in_specs=[
            pl.BlockSpec((1, QKV_ROWS, d), lambda i, t: (i, t, 0)),
            row, vec, vec,
            pl.BlockSpec((d, 2 * d), lambda i, t: (0, 0), **resident),
            pl.BlockSpec((d, d), lambda i, t: (0, 0), **resident),
        ],
        out_specs=[
            pl.BlockSpec((1, QKV_ROWS, d), lambda i, t: (i, t, 0)),
            pl.BlockSpec((1, QKV_ROWS, d), lambda i, t: (i, t, 0)),
            pl.BlockSpec((1, d, QKV_ROWS), lambda i, t: (i, 0, t)),
            pl.BlockSpec((1, km_rows, d), lambda i, t: (i, t // steps_per_km, 0)),
        ],
        out_shape=[
            jax.ShapeDtypeStruct((b, s, d), BF16),
            jax.ShapeDtypeStruct((b, s, d), BF16),
            jax.ShapeDtypeStruct((b, d, s), BF16),
            jax.ShapeDtypeStruct((b, nb, d), F32),
        ],
        compiler_params=_params(("parallel", "arbitrary")),
        name="qkv_proj",
    )(x, gain.reshape(1, d), shift.reshape(b, 1, d), scale.reshape(b, 1, d), w_qk, w_vt)


def _bias_kernel(rb_ref, o_ref):
    head = pl.program_id(0)
    kk = lax.broadcasted_iota(jnp.int32, (MOBA_BLOCK, MOBA_BLOCK), 0)
    qq = lax.broadcasted_iota(jnp.int32, (MOBA_BLOCK, MOBA_BLOCK), 1)
    max_exact = NUM_BUCKETS // 2
    for off in range(BIAS_TILES):
        dist = MOBA_BLOCK * off + qq - kk
        dpos = jnp.maximum(dist, 0)
        nf = jnp.maximum(dpos, 1).astype(F32)
        large = max_exact + (jnp.log(nf / max_exact) / math.log(MAX_DISTANCE / max_exact)
                             * (NUM_BUCKETS - max_exact)).astype(jnp.int32)
        large = jnp.minimum(large, NUM_BUCKETS - 1)
        bucket = jnp.where(dpos < max_exact, dpos, large)
        val = jnp.zeros((MOBA_BLOCK, MOBA_BLOCK), F32)
        for bkt in range(NUM_BUCKETS):
            val = jnp.where(bucket == bkt, rb_ref[bkt, head], val)
        val = val * LOG2E
        if off == 0:
            val = jnp.where(dist >= 0, val, -jnp.inf)
        o_ref[0, off] = val


def _bias_tiles(rel_bias):
    heads = rel_bias.shape[1]
    return pl.pallas_call(
        _bias_kernel,
        grid=(heads,),
        in_specs=[pl.BlockSpec(memory_space=pltpu.SMEM)],
        out_specs=pl.BlockSpec((1, BIAS_TILES, MOBA_BLOCK, MOBA_BLOCK),
                               lambda h: (h, 0, 0, 0)),
        out_shape=jax.ShapeDtypeStruct((heads, BIAS_TILES, MOBA_BLOCK, MOBA_BLOCK), F32),
        compiler_params=_params(("parallel",)),
        name="t5_bias_tiles",
    )(rel_bias)


def _attn_kernel(far_ref, q_ref, k_ref, vt_ref, km_ref, bias_ref, o_ref, mask_scr, sa_ref, sb_ref):
    hp = pl.program_id(1)
    i = pl.program_id(2)
    nb = km_ref.shape[1]
    neg = -jnp.inf
    q_t = q_ref[0].astype(F32).T
    dim_row = lax.broadcasted_iota(jnp.int32, q_t.shape, 0)
    blk = lax.broadcasted_iota(jnp.int32, (nb, MOBA_BLOCK), 0)
    blk_f = blk.astype(F32)
    k_mean = km_ref[0].astype(BF16)
    own_group = i // KEY_GROUP
    heads = range(HEADS_PER_STEP)
    ones_rows = jnp.ones((DENOM_ROWS, MOBA_BLOCK), BF16)

    def key_block(j):
        return k_ref[0, pl.ds(pl.multiple_of(j * MOBA_BLOCK, MOBA_BLOCK), MOBA_BLOCK), :]

    def value_rows(hh, j):
        v_t = vt_ref[0, hh * HEAD_DIM:(hh + 1) * HEAD_DIM,
                     pl.ds(pl.multiple_of(j * MOBA_BLOCK, MOBA_BLOCK), MOBA_BLOCK)]
        return jnp.concatenate([v_t, ones_rows], axis=0)

    def near_bias(hh, j, valid):
        off = i - j
        sel_row = jnp.where(off > 0, mask_scr[hh, 1, pl.ds(j, 1), :], neg)
        sel_row = jnp.where(off == 0, 0.0, sel_row)
        sel_row = jnp.where(valid, sel_row, neg)
        return bias_ref[hh, jnp.clip(off, 0, NEAR_OFFSETS)] + sel_row

    def far_bias_row(hh, j, valid):
        del valid
        return mask_scr[hh, 0, pl.ds(j, 1), :]

    def produce(group, dst_ref, bias_fn, valid):
        first_block = group * KEY_GROUP
        col_max = []
        for hh in heads:
            best = None
            for n in range(KEY_GROUP):
                j = first_block + n
                s = _dot(key_block(j), q_heads[hh]) + bias_fn(hh, j, valid)
                dst_ref[hh, n] = s
                tile_max = jnp.max(s, axis=0, keepdims=True)
                best = tile_max if best is None else jnp.maximum(best, tile_max)
            col_max.append(best)
        return tuple(col_max)

    def consume(states, group, src_ref, col_max):
        first_block = group * KEY_GROUP
        out = []
        for hh in heads:
            m, acc = states[hh]
            m_new = jnp.maximum(m, col_max[hh])
            acc = jnp.exp2(m - m_new) * acc
            for n in range(KEY_GROUP):
                p = jnp.exp2(src_ref[hh, n] - m_new).astype(BF16)
                acc = acc + _dot(value_rows(hh, first_block + n), p)
            out.append((m_new, acc))
        return tuple(out)

    q_heads = []
    for hh in heads:
        own = (dim_row >= hh * HEAD_DIM) & (dim_row < (hh + 1) * HEAD_DIM)
        q_h = jnp.where(own, q_t, 0.0).astype(BF16)
        q_heads.append(q_h)
        gate = jnp.where(blk < i, _dot(k_mean, q_h), neg)
        chosen = jnp.zeros((nb, MOBA_BLOCK), F32)
        for _ in range(MOBA_TOPK):
            best = jnp.max(gate, axis=0, keepdims=True)
            first = jnp.min(jnp.where(gate == best, blk_f, float(nb)), axis=0, keepdims=True)
            hit = blk_f == first
            chosen = jnp.where(hit & (best > neg), 1.0, chosen)
            gate = jnp.where(hit, neg, gate)
        far_bias = far_ref[hp * HEADS_PER_STEP + hh] * LOG2E
        mask_scr[hh, 0] = jnp.where(chosen > 0.0, far_bias, neg)
        mask_scr[hh, 1] = jnp.where(chosen > 0.0, 0.0, neg)

    prev_group = jnp.maximum(own_group - 1, 0)
    n_far = jnp.maximum(own_group - 1, 0)
    init = (jnp.full((1, MOBA_BLOCK), neg, F32),
            jnp.zeros((HEAD_DIM + DENOM_ROWS, MOBA_BLOCK), F32))
    max_own = produce(own_group, sa_ref, near_bias, True)
    max_prev = produce(prev_group, sb_ref, near_bias, own_group > 0)
    states = consume((init,) * HEADS_PER_STEP, own_group, sa_ref, max_own)
    max_far = produce(jnp.maximum(n_far - 1, 0), sa_ref, far_bias_row, True)
    states = consume(states, prev_group, sb_ref, max_prev)

    def far_step(t, carry):
        states, col_max = carry
        group = n_far - 1 - t
        ahead = jnp.maximum(group - 1, 0)

        def from_a(args):
            states, col_max = args
            nxt = produce(ahead, sb_ref, far_bias_row, True)
            return consume(states, group, sa_ref, col_max), nxt

        def from_b(args):
            states, col_max = args
            nxt = produce(ahead, sa_ref, far_bias_row, True)
            return consume(states, group, sb_ref, col_max), nxt

        return lax.cond(t % 2 == 0, from_a, from_b, (states, col_max))

    states, _ = lax.fori_loop(0, n_far, far_step, (states, max_far))
    out_t = jnp.concatenate(
        [acc[:HEAD_DIM] * (1.0 / acc[HEAD_DIM:HEAD_DIM + 1]) for (_, acc) in states], axis=0)
    o_ref[0] = out_t.T.astype(BF16)


def _attention(q, k, v_t, k_mean, bias_tiles, far_bias):
    b, s, d = q.shape
    nb = s // MOBA_BLOCK
    head_pairs = d // V7X_LANES
    return pl.pallas_call(
        _attn_kernel,
        grid=(b, head_pairs, nb),
        in_specs=[
            pl.BlockSpec(memory_space=pltpu.SMEM),
            pl.BlockSpec((1, MOBA_BLOCK, V7X_LANES), lambda bi, hp, i: (bi, i, hp)),
            pl.BlockSpec((1, s, V7X_LANES), lambda bi, hp, i: (bi, 0, hp)),
            pl.BlockSpec((1, V7X_LANES, s), lambda bi, hp, i: (bi, hp, 0)),
            pl.BlockSpec((1, nb, V7X_LANES), lambda bi, hp, i: (bi, 0, hp)),
            pl.BlockSpec((HEADS_PER_STEP, BIAS_TILES, MOBA_BLOCK, MOBA_BLOCK),
                         lambda bi, hp, i: (hp, 0, 0, 0)),
        ],
        out_specs=pl.BlockSpec((1, MOBA_BLOCK, V7X_LANES), lambda bi, hp, i: (bi, i, hp)),
        out_shape=jax.ShapeDtypeStruct((b, s, d), BF16),
        scratch_shapes=[pltpu.VMEM((HEADS_PER_STEP, 2, nb, MOBA_BLOCK), F32)]
        + [pltpu.VMEM((HEADS_PER_STEP, KEY_GROUP, MOBA_BLOCK, MOBA_BLOCK), F32)] * 2,
        compiler_params=_params(("parallel", "parallel", "arbitrary")),
        name="moba_attention",
    )(far_bias, q, k, v_t, k_mean, bias_tiles)


def _oproj_kernel(x_ref, o_ref_in, w_ref, gate_ref, out_ref):
    out_ref[0] = x_ref[0] + gate_ref[0] * _dot(o_ref_in[0], w_ref[...])


def _oproj_layer(x, o, w_o, gate):
    b, s, d = x.shape
    tile = pl.BlockSpec((1, MLP_ROWS, d), lambda i, t: (i, t, 0))
    return pl.pallas_call(
        _oproj_kernel,
        grid=(b, s // MLP_ROWS),
        in_specs=[
            tile, tile,
            pl.BlockSpec((d, d), lambda i, t: (0, 0), pipeline_mode=pl.Buffered(1)),
            pl.BlockSpec((1, 1, d), lambda i, t: (i, 0, 0)),
        ],
        out_specs=tile,
        out_shape=jax.ShapeDtypeStruct((b, s, d), F32),
        compiler_params=_params(("parallel", "parallel")),
        name="attn_out_proj",
    )(x, o, w_o.astype(BF16), gate.reshape(b, 1, d))


def kernel(x, c, rel_bias, w_mod, b_mod, norm_mix, norm_mlp, w_pool, pool_scale,
           w_qkv, w_o, w_up, w_down, norm_final):
    b, s, d = x.shape
    assert d % V7X_LANES == 0 and s % MLP_ROWS == 0 and s % (V7X_SUBLANES * MOBA_BLOCK) == 0
    assert w_mod.shape[0] == 2 and w_mod.shape[2] % MOD_COLS == 0
    mod = _modulation(c, w_mod, b_mod)
    sh1, sc1, g1, sh2, sc2, g2 = (mod[:, :, n * d:(n + 1) * d] for n in range(6))

    x = _pool_layer(x, norm_mix[0], sh1[0], sc1[0], g1[0], w_pool[0], pool_scale[0])
    x = _mlp_layer(x, norm_mlp[0], sh2[0], sc2[0], g2[0], w_up[0], w_down[0], norm_final, False)

    q, k, v_t, k_mean = _qkv_layer(x, norm_mix[1], sh1[1], sc1[1], w_qkv[0])
    o = _attention(q, k, v_t, k_mean, _bias_tiles(rel_bias), rel_bias[NUM_BUCKETS - 1])
    x = _oproj_layer(x, o, w_o[0], g1[1])
    return _mlp_layer(x, norm_mlp[1], sh2[1], sc2[1], g2[1], w_up[1], w_down[1], norm_final, True)
```

```python
import functools
import math

import jax
import jax.numpy as jnp
from jax import lax
from jax.experimental import pallas as pl
from jax.experimental.pallas import tpu as pltpu

F32 = jnp.float32
BF16 = jnp.bfloat16

HEAD_DIM = 64
MOBA_BLOCK = 256
MOBA_TOPK = 3
NUM_BUCKETS = 32
MAX_DISTANCE = 1024
POOL_WINDOWS = (2, 4, 8, 16)
EPS = 1e-6

V7X_LANES = 128
V7X_SUBLANES = 8
V7X_VMEM_LIMIT_BYTES = 56 * 1024 * 1024

NEAR_OFFSETS = 5
BIAS_TILES = NEAR_OFFSETS + 1
KEY_GROUP = 4
DENOM_ROWS = 16
LOG2E = math.log2(math.e)
HEADS_PER_SLAB = V7X_LANES // HEAD_DIM
HEADS_PER_STEP = 2 * HEADS_PER_SLAB
POOL_HALO = max(POOL_WINDOWS)

POOL_ROWS = 512
MLP_ROWS = 512
MLP_FF_CHUNK = 512
QKV_ROWS = 512
SELECT_COLS = 2048
MOD_COLS = 1536


def _params(semantics):
    return pltpu.CompilerParams(dimension_semantics=semantics,
                                vmem_limit_bytes=V7X_VMEM_LIMIT_BYTES)


def _dot(a, b):
    return jnp.dot(a, b, preferred_element_type=F32)


def _rms_modulate(x, gain, shift, scale):
    y = x * lax.rsqrt(jnp.mean(x * x, axis=-1, keepdims=True) + EPS)
    return (y * gain) * (1.0 + scale) + shift


def _mod_kernel(c_ref, w_ref, b_ref, o_ref):
    c = c_ref[...]
    a = c * (1.0 / (1.0 + jnp.exp(-c)))
    w = w_ref[0]
    a_hi = a.astype(BF16)
    a_lo = (a - a_hi.astype(F32)).astype(BF16)
    w_hi = w.astype(BF16)
    w_lo = (w - w_hi.astype(F32)).astype(BF16)
    acc = _dot(a_hi, w_hi) + _dot(a_hi, w_lo) + _dot(a_lo, w_hi)
    o_ref[0] = acc + b_ref[0]


def _modulation(c, w_mod, b_mod):
    depth, d, n = w_mod.shape
    b = c.shape[0]
    rows = -(-b // V7X_SUBLANES) * V7X_SUBLANES
    c_pad = jnp.pad(c, ((0, rows - b), (0, 0)))
    out = pl.pallas_call(
        _mod_kernel,
        grid=(depth, n // MOD_COLS),
        in_specs=[
            pl.BlockSpec((rows, d), lambda i, j: (0, 0)),
            pl.BlockSpec((1, d, MOD_COLS), lambda i, j: (i, 0, j)),
            pl.BlockSpec((1, 1, MOD_COLS), lambda i, j: (i, 0, j)),
        ],
        out_specs=pl.BlockSpec((1, rows, MOD_COLS), lambda i, j: (i, 0, j)),
        out_shape=jax.ShapeDtypeStruct((depth, rows, n), F32),
        compiler_params=_params(("parallel", "parallel")),
        name="modulation",
    )(c_pad, w_mod, b_mod.reshape(depth, 1, n))
    return out[:, :b]


def _pool_kernel(x_ref, halo_ref, gain_ref, sh_ref, sc_ref, gate_ref, w_ref, ps_ref, o_ref):
    t = pl.program_id(1)
    rows = x_ref.shape[1]
    group = w_ref.shape[1]
    x = x_ref[0]
    gain, shift, scale = gain_ref[...], sh_ref[0], sc_ref[0]
    h = _rms_modulate(x, gain, shift, scale)
    h_prev = _rms_modulate(halo_ref[0], gain, shift, scale)
    h_prev = jnp.where(t > 0, h_prev, 0.0)
    h_ext = jnp.concatenate([h_prev, h], axis=0)
    pos = t * rows + lax.broadcasted_iota(jnp.int32, (rows, 1), 0)
    ys = []
    for g, window in enumerate(POOL_WINDOWS):
        cols = slice(g * group, (g + 1) * group)
        s = h_ext[:, cols]
        step = 1
        while step < window:
            s = s + pltpu.roll(s, step, axis=0)
            step *= 2
        inv_cnt = 1.0 / jnp.minimum(pos + 1, window).astype(F32)
        pooled = s[POOL_HALO:] * inv_cnt - h[:, cols]
        ys.append(_dot(pooled.astype(BF16), w_ref[g]))
    y = jnp.concatenate(ys, axis=-1) * ps_ref[...]
    o_ref[0] = x + gate_ref[0] * y


def _pool_layer(x, gain, shift, scale, gate, w_pool, pool_scale):
    b, s, d = x.shape
    groups, group, _ = w_pool.shape
    halo_blocks = POOL_ROWS // POOL_HALO
    vec = pl.BlockSpec((1, 1, d), lambda i, t: (i, 0, 0))
    row = pl.BlockSpec((1, d), lambda i, t: (0, 0))
    return pl.pallas_call(
        _pool_kernel,
        grid=(b, s // POOL_ROWS),
        in_specs=[
            pl.BlockSpec((1, POOL_ROWS, d), lambda i, t: (i, t, 0)),
            pl.BlockSpec((1, POOL_HALO, d),
                         lambda i, t: (i, jnp.maximum(t * halo_blocks - 1, 0), 0)),
            row, vec, vec, vec,
            pl.BlockSpec((groups, group, group), lambda i, t: (0, 0, 0)),
            row,
        ],
        out_specs=pl.BlockSpec((1, POOL_ROWS, d), lambda i, t: (i, t, 0)),
        out_shape=jax.ShapeDtypeStruct((b, s, d), F32),
        compiler_params=_params(("parallel", "parallel")),
        name="pool_mixer",
    )(x, x, gain.reshape(1, d), shift.reshape(b, 1, d), scale.reshape(b, 1, d),
      gate.reshape(b, 1, d), w_pool.astype(BF16), pool_scale.reshape(1, d))


def _mlp_kernel(x_ref, gain_ref, sh_ref, sc_ref, gate_ref, wup_ref, wdn_ref, fin_ref,
                o_ref, h_scr, acc_scr, *, final_norm):
    x = x_ref[0]
    h_scr[...] = _rms_modulate(x, gain_ref[...], sh_ref[0], sc_ref[0]).astype(BF16)
    acc_scr[...] = jnp.zeros_like(acc_scr)
    n_chunks = wup_ref.shape[1] // MLP_FF_CHUNK

    def chunk(f, carry):
        f0 = pl.multiple_of(f * MLP_FF_CHUNK, MLP_FF_CHUNK)
        up = _dot(h_scr[...], wup_ref[:, pl.ds(f0, MLP_FF_CHUNK)])
        a = jnp.maximum(up, 0.0)
        acc_scr[...] += _dot((a * a).astype(BF16), wdn_ref[pl.ds(f0, MLP_FF_CHUNK), :])
        return carry

    lax.fori_loop(0, n_chunks, chunk, 0, unroll=True)
    out = x + gate_ref[0] * acc_scr[...]
    if final_norm:
        out = out * lax.rsqrt(jnp.mean(out * out, axis=-1, keepdims=True) + EPS)
        out = out * fin_ref[...]
    o_ref[0] = out


def _mlp_layer(x, gain, shift, scale, gate, w_up, w_down, norm_final, final_norm):
    b, s, d = x.shape
    ff = w_up.shape[1]
    vec = pl.BlockSpec((1, 1, d), lambda i, t: (i, 0, 0))
    row = pl.BlockSpec((1, d), lambda i, t: (0, 0))
    resident = dict(pipeline_mode=pl.Buffered(1))
    return pl.pallas_call(
        functools.partial(_mlp_kernel, final_norm=final_norm),
        grid=(b, s // MLP_ROWS),
        in_specs=[
            pl.BlockSpec((1, MLP_ROWS, d), lambda i, t: (i, t, 0)),
            row, vec, vec, vec,
            pl.BlockSpec((d, ff), lambda i, t: (0, 0), **resident),
            pl.BlockSpec((ff, d), lambda i, t: (0, 0), **resident),
            row,
        ],
        out_specs=pl.BlockSpec((1, MLP_ROWS, d), lambda i, t: (i, t, 0)),
        out_shape=jax.ShapeDtypeStruct((b, s, d), F32),
        scratch_shapes=[pltpu.VMEM((MLP_ROWS, d), BF16), pltpu.VMEM((MLP_ROWS, d), F32)],
        compiler_params=_params(("parallel", "parallel")),
        name="mlp_final" if final_norm else "mlp",
    )(x, gain.reshape(1, d), shift.reshape(b, 1, d), scale.reshape(b, 1, d),
      gate.reshape(b, 1, d), w_up.astype(BF16), w_down.astype(BF16), norm_final.reshape(1, d))


def _qkv_kernel(x_ref, gain_ref, sh_ref, sc_ref, wk_ref, wqvt_ref,
                qt_ref, k_ref, vt_ref, km_ref):
    t = pl.program_id(1)
    d = x_ref.shape[2]
    rows = x_ref.shape[1]
    blocks = rows // MOBA_BLOCK
    h = _rms_modulate(x_ref[0], gain_ref[...], sh_ref[0], sc_ref[0]).astype(BF16)
    k = _dot(h, wk_ref[...])
    k_ref[0] = k.astype(BF16)
    k_mean = jnp.mean(k.reshape(blocks, MOBA_BLOCK, d), axis=1)
    steps = km_ref.shape[1] // blocks
    for step in range(steps):
        @pl.when(t % steps == step)
        def _():
            km_ref[0, step * blocks:(step + 1) * blocks, :] = k_mean
    qvt = lax.dot_general(wqvt_ref[...], h, (((1,), (1,)), ((), ())), preferred_element_type=F32)
    qt_ref[0] = (qvt[:d] * (HEAD_DIM ** -0.5 * LOG2E)).astype(BF16)
    vt_ref[0] = qvt[d:].astype(BF16)


def _qkv_layer(x, gain, shift, scale, w_qkv):
    b, s, d = x.shape
    nb = s // MOBA_BLOCK
    km_rows = V7X_SUBLANES
    steps_per_km = km_rows * MOBA_BLOCK // QKV_ROWS
    vec = pl.BlockSpec((1, 1, d), lambda i, t: (i, 0, 0))
    row = pl.BlockSpec((1, d), lambda i, t: (0, 0))
    resident = dict(pipeline_mode=pl.Buffered(1))
    w_k = w_qkv[:, d:2 * d].astype(BF16)
    w_qvt = jnp.concatenate([w_qkv[:, :d], w_qkv[:, 2 * d:]], axis=1).T.astype(BF16)
    transposed = pl.BlockSpec((1, d, QKV_ROWS), lambda i, t: (i, 0, t))
    return pl.pallas_call(
        _qkv_kernel,
        grid=(b, s // QKV_ROWS),
        in_specs=[
            pl.BlockSpec((1, QKV_ROWS, d), lambda i, t: (i, t, 0)),
            row, vec, vec,
            pl.BlockSpec((d, d), lambda i, t: (0, 0), **resident),
            pl.BlockSpec((2 * d, d), lambda i, t: (0, 0), **resident),
        ],
        out_specs=[
            transposed,
            pl.BlockSpec((1, QKV_ROWS, d), lambda i, t: (i, t, 0)),
            transposed,
            pl.BlockSpec((1, km_rows, d), lambda i, t: (i, t // steps_per_km, 0)),
        ],
        out_shape=[
            jax.ShapeDtypeStruct((b, d, s), BF16),
            jax.ShapeDtypeStruct((b, s, d), BF16),
            jax.ShapeDtypeStruct((b, d, s), BF16),
            jax.ShapeDtypeStruct((b, nb, d), F32),
        ],
        compiler_params=_params(("parallel", "arbitrary")),
        name="qkv_proj",
    )(x, gain.reshape(1, d), shift.reshape(b, 1, d), scale.reshape(b, 1, d), w_k, w_qvt)


def _select_kernel(qt_ref, km_ref, o_ref):
    t = pl.program_id(2)
    nb = km_ref.shape[1]
    cols = qt_ref.shape[2]
    lane = lax.broadcasted_iota(jnp.int32, km_ref.shape[1:], 1)
    blk = lax.broadcasted_iota(jnp.int32, (nb, cols), 0)
    own_blk = (t * cols + lax.broadcasted_iota(jnp.int32, (nb, cols), 1)) // MOBA_BLOCK
    blk_f = blk.astype(F32)
    neg = -jnp.inf
    for hh in range(HEADS_PER_SLAB):
        in_head = (lane >= hh * HEAD_DIM) & (lane < (hh + 1) * HEAD_DIM)
        k_mean = jnp.where(in_head, km_ref[0], 0.0).astype(BF16)
        gate = jnp.where(blk < own_blk, _dot(k_mean, qt_ref[0]), neg)
        chosen = jnp.full((nb, cols), neg, F32)
        for _ in range(MOBA_TOPK):
            best = jnp.max(gate, axis=0, keepdims=True)
            first = jnp.min(jnp.where(gate == best, blk_f, float(nb)), axis=0, keepdims=True)
            hit = blk_f == first
            chosen = jnp.where(hit & (best > neg), 0.0, chosen)
            gate = jnp.where(hit, neg, gate)
        o_ref[0, hh] = chosen


def _select_blocks(q_t, k_mean):
    b, d, s = q_t.shape
    nb = k_mean.shape[1]
    slabs = d // V7X_LANES
    return pl.pallas_call(
        _select_kernel,
        grid=(b, slabs, s // SELECT_COLS),
        in_specs=[
            pl.BlockSpec((1, V7X_LANES, SELECT_COLS), lambda bi, sl, t: (bi, sl, t)),
            pl.BlockSpec((1, nb, V7X_LANES), lambda bi, sl, t: (bi, 0, sl)),
        ],
        out_specs=pl.BlockSpec((1, HEADS_PER_SLAB, nb, SELECT_COLS),
                               lambda bi, sl, t: (bi, sl, 0, t)),
        out_shape=jax.ShapeDtypeStruct((b, slabs * HEADS_PER_SLAB, nb, s), F32),
        compiler_params=_params(("parallel", "parallel", "parallel")),
        name="moba_select",
    )(q_t, k_mean)


def _bias_kernel(rb_ref, o_ref):
    head = pl.program_id(0)
    kk = lax.broadcasted_iota(jnp.int32, (MOBA_BLOCK, MOBA_BLOCK), 0)
    qq = lax.broadcasted_iota(jnp.int32, (MOBA_BLOCK, MOBA_BLOCK), 1)
    max_exact = NUM_BUCKETS // 2
    for off in range(BIAS_TILES):
        dist = MOBA_BLOCK * off + qq - kk
        dpos = jnp.maximum(dist, 0)
        nf = jnp.maximum(dpos, 1).astype(F32)
        large = max_exact + (jnp.log(nf / max_exact) / math.log(MAX_DISTANCE / max_exact)
                             * (NUM_BUCKETS - max_exact)).astype(jnp.int32)
        large = jnp.minimum(large, NUM_BUCKETS - 1)
        bucket = jnp.where(dpos < max_exact, dpos, large)
        val = jnp.zeros((MOBA_BLOCK, MOBA_BLOCK), F32)
        for bkt in range(NUM_BUCKETS):
            val = jnp.where(bucket == bkt, rb_ref[bkt, head], val)
        val = val * LOG2E
        if off == 0:
            val = jnp.where(dist >= 0, val, -jnp.inf)
        o_ref[0, off] = val


def _bias_tiles(rel_bias):
    heads = rel_bias.shape[1]
    return pl.pallas_call(
        _bias_kernel,
        grid=(heads,),
        in_specs=[pl.BlockSpec(memory_space=pltpu.SMEM)],
        out_specs=pl.BlockSpec((1, BIAS_TILES, MOBA_BLOCK, MOBA_BLOCK),
                               lambda h: (h, 0, 0, 0)),
        out_shape=jax.ShapeDtypeStruct((heads, BIAS_TILES, MOBA_BLOCK, MOBA_BLOCK), F32),
        compiler_params=_params(("parallel",)),
        name="t5_bias_tiles",
    )(rel_bias)


def _attn_kernel(far_ref, qt_ref, k_ref, vt_ref, sel_ref, bias_ref, o_ref, sa_ref, sb_ref):
    hq = pl.program_id(1)
    i = pl.program_id(2)
    neg = -jnp.inf
    own_group = i // KEY_GROUP
    n_far = jnp.maximum(own_group - 1, 0)
    heads = range(HEADS_PER_STEP)
    ones_rows = jnp.ones((DENOM_ROWS, MOBA_BLOCK), BF16)
    dim_row = lax.broadcasted_iota(jnp.int32, (V7X_LANES, MOBA_BLOCK), 0)

    def slab_of(hh):
        slab = hh // HEADS_PER_SLAB
        return slice(slab * V7X_LANES, (slab + 1) * V7X_LANES)

    q_heads = []
    for hh in heads:
        lo = (hh % HEADS_PER_SLAB) * HEAD_DIM
        q_slab = qt_ref[0, slab_of(hh), :]
        own = (dim_row >= lo) & (dim_row < lo + HEAD_DIM)
        q_heads.append(jnp.where(own, q_slab, jnp.zeros_like(q_slab)))

    def key_block(hh, j):
        return k_ref[0, pl.ds(pl.multiple_of(j * MOBA_BLOCK, MOBA_BLOCK), MOBA_BLOCK), slab_of(hh)]

    def value_rows(hh, j):
        v_t = vt_ref[0, hh * HEAD_DIM:(hh + 1) * HEAD_DIM,
                     pl.ds(pl.multiple_of(j * MOBA_BLOCK, MOBA_BLOCK), MOBA_BLOCK)]
        return jnp.concatenate([v_t, ones_rows], axis=0)

    def near_bias(hh, j):
        off = i - j
        sel_row = jnp.where(off > 0, sel_ref[0, hh, pl.ds(j, 1), :], neg)
        sel_row = jnp.where(off == 0, 0.0, sel_row)
        return bias_ref[hh, jnp.clip(off, 0, NEAR_OFFSETS)] + sel_row

    def far_bias(hh, j):
        return sel_ref[0, hh, pl.ds(j, 1), :] + far_ref[hq * HEADS_PER_STEP + hh] * LOG2E

    def score_tile(hh, j, n, dst_ref, bias_fn):
        s = _dot(key_block(hh, j), q_heads[hh]) + bias_fn(hh, j)
        dst_ref[hh, n] = s
        return jnp.max(s, axis=0, keepdims=True)

    def step(states, col_max, group, src_ref, ahead=None):
        first_block = group * KEY_GROUP
        new_states, new_max = [], []
        for hh in heads:
            m, acc = states[hh]
            m_new = jnp.maximum(m, col_max[hh])
            acc = jnp.exp2(m - m_new) * acc
            best = None
            for n in range(KEY_GROUP):
                if ahead is not None:
                    a_group, dst_ref, bias_fn = ahead
                    tile_max = score_tile(hh, a_group * KEY_GROUP + n, n, dst_ref, bias_fn)
                    best = tile_max if best is None else jnp.maximum(best, tile_max)
                p = jnp.exp2(src_ref[hh, n] - m_new).astype(BF16)
                acc = acc + _dot(value_rows(hh, first_block + n), p)
            new_states.append((m_new, acc))
            new_max.append(col_max[hh] if best is None else best)
        return tuple(new_states), tuple(new_max)

    init = ((jnp.full((1, MOBA_BLOCK), neg, F32),
             jnp.zeros((HEAD_DIM + DENOM_ROWS, MOBA_BLOCK), F32)),) * HEADS_PER_STEP
    max_own = []
    for hh in heads:
        tile_max = [score_tile(hh, own_group * KEY_GROUP + n, n, sa_ref, near_bias)
                    for n in range(KEY_GROUP)]
        max_own.append(functools.reduce(jnp.maximum, tile_max))
    max_own = tuple(max_own)

    def own_then_prev(max_own):
        prev_group = own_group - 1
        states, max_prev = step(init, max_own, own_group, sa_ref, (prev_group, sb_ref, near_bias))
        return lax.cond(
            n_far > 0,
            lambda c: step(c[0], c[1], prev_group, sb_ref, (n_far - 1, sa_ref, far_bias)),
            lambda c: step(c[0], c[1], prev_group, sb_ref),
            (states, max_prev))

    states, max_far = lax.cond(own_group > 0, own_then_prev,
                               lambda max_own: step(init, max_own, own_group, sa_ref), max_own)

    def far_step(t, carry):
        group = n_far - 1 - t
        branches = [
            lambda c: step(c[0], c[1], group, sa_ref, (group - 1, sb_ref, far_bias)),
            lambda c: step(c[0], c[1], group, sb_ref, (group - 1, sa_ref, far_bias)),
            lambda c: step(c[0], c[1], group, sa_ref),
            lambda c: step(c[0], c[1], group, sb_ref),
        ]
        is_last = (t == n_far - 1).astype(jnp.int32)
        return lax.switch(t % 2 + 2 * is_last, branches, carry)

    states, _ = lax.fori_loop(0, n_far, far_step, (states, max_far))
    out_t = jnp.concatenate(
        [acc[:HEAD_DIM] * (1.0 / acc[HEAD_DIM:HEAD_DIM + 1]) for (_, acc) in states], axis=0)
    o_ref[0] = out_t.T.astype(BF16)


def _attention(q_t, k, v_t, select, bias_tiles, far_bias):
    b, d, s = q_t.shape
    nb = s // MOBA_BLOCK
    width = HEADS_PER_STEP * HEAD_DIM
    return pl.pallas_call(
        _attn_kernel,
        grid=(b, d // width, nb),
        in_specs=[
            pl.BlockSpec(memory_space=pltpu.SMEM),
            pl.BlockSpec((1, width, MOBA_BLOCK), lambda bi, hq, i: (bi, hq, i)),
            pl.BlockSpec((1, s, width), lambda bi, hq, i: (bi, 0, hq)),
            pl.BlockSpec((1, width, s), lambda bi, hq, i: (bi, hq, 0)),
            pl.BlockSpec((1, HEADS_PER_STEP, nb, MOBA_BLOCK), lambda bi, hq, i: (bi, hq, 0, i)),
            pl.BlockSpec((HEADS_PER_STEP, BIAS_TILES, MOBA_BLOCK, MOBA_BLOCK),
                         lambda bi, hq, i: (hq, 0, 0, 0)),
        ],
        out_specs=pl.BlockSpec((1, MOBA_BLOCK, width), lambda bi, hq, i: (bi, i, hq)),
        out_shape=jax.ShapeDtypeStruct((b, s, d), BF16),
        scratch_shapes=[pltpu.VMEM((HEADS_PER_STEP, KEY_GROUP, MOBA_BLOCK, MOBA_BLOCK), F32)] * 2,
        compiler_params=_params(("parallel", "parallel", "arbitrary")),
        name="moba_attention",
    )(far_bias, q_t, k, v_t, select, bias_tiles)


def _oproj_kernel(x_ref, o_ref_in, w_ref, gate_ref, out_ref):
    out_ref[0] = x_ref[0] + gate_ref[0] * _dot(o_ref_in[0], w_ref[...])


def _oproj_layer(x, o, w_o, gate):
    b, s, d = x.shape
    tile = pl.BlockSpec((1, MLP_ROWS, d), lambda i, t: (i, t, 0))
    return pl.pallas_call(
        _oproj_kernel,
        grid=(b, s // MLP_ROWS),
        in_specs=[
            tile, tile,
            pl.BlockSpec((d, d), lambda i, t: (0, 0), pipeline_mode=pl.Buffered(1)),
            pl.BlockSpec((1, 1, d), lambda i, t: (i, 0, 0)),
        ],
        out_specs=tile,
        out_shape=jax.ShapeDtypeStruct((b, s, d), F32),
        compiler_params=_params(("parallel", "parallel")),
        name="attn_out_proj",
    )(x, o, w_o.astype(BF16), gate.reshape(b, 1, d))


def kernel(x, c, rel_bias, w_mod, b_mod, norm_mix, norm_mlp, w_pool, pool_scale,
           w_qkv, w_o, w_up, w_down, norm_final):
    b, s, d = x.shape
    assert d % (HEADS_PER_STEP * HEAD_DIM) == 0 and s % MLP_ROWS == 0
    assert s % SELECT_COLS == 0 and (s // MOBA_BLOCK) % KEY_GROUP == 0
    assert w_mod.shape[0] == 2 and w_mod.shape[2] % MOD_COLS == 0
    mod = _modulation(c, w_mod, b_mod)
    sh1, sc1, g1, sh2, sc2, g2 = (mod[:, :, n * d:(n + 1) * d] for n in range(6))

    x = _pool_layer(x, norm_mix[0], sh1[0], sc1[0], g1[0], w_pool[0], pool_scale[0])
    x = _mlp_layer(x, norm_mlp[0], sh2[0], sc2[0], g2[0], w_up[0], w_down[0], norm_final, False)

    q_t, k, v_t, k_mean = _qkv_layer(x, norm_mix[1], sh1[1], sc1[1], w_qkv[0])
    select = _select_blocks(q_t, k_mean)
    o = _attention(q_t, k, v_t, select, _bias_tiles(rel_bias), rel_bias[NUM_BUCKETS - 1])
    x = _oproj_layer(x, o, w_o[0], g1[1])
    return _mlp_layer(x, norm_mlp[1], sh2[1], sc2[1], g2[1], w_up[1], w_down[1], norm_final, True)
```

```python
import functools
import math

import jax
import jax.numpy as jnp
from jax import lax
from jax.experimental import pallas as pl
from jax.experimental.pallas import tpu as pltpu

F32 = jnp.float32
BF16 = jnp.bfloat16

HEAD_DIM = 64
MOBA_BLOCK = 256
MOBA_TOPK = 3
NUM_BUCKETS = 32
MAX_DISTANCE = 1024
POOL_WINDOWS = (2, 4, 8, 16)
EPS = 1e-6

V7X_LANES = 128
V7X_SUBLANES = 8
V7X_VMEM_LIMIT_BYTES = 56 * 1024 * 1024

NEAR_OFFSETS = 5
BIAS_TILES = NEAR_OFFSETS + 1
KEY_GROUP = 4
DENOM_ROWS = 16
SCORE_LOOKAHEAD = 8
LOG2E = math.log2(math.e)
HEADS_PER_SLAB = V7X_LANES // HEAD_DIM
HEADS_PER_STEP = 2 * HEADS_PER_SLAB
POOL_HALO = max(POOL_WINDOWS)

POOL_ROWS = 512
MLP_ROWS = 512
MLP_FF_CHUNK = 512
QKV_ROWS = 512
SELECT_COLS = 2048
MOD_COLS = 1536


def _params(semantics):
    return pltpu.CompilerParams(dimension_semantics=semantics,
                                vmem_limit_bytes=V7X_VMEM_LIMIT_BYTES)


def _dot(a, b):
    return jnp.dot(a, b, preferred_element_type=F32)


def _rms_modulate(x, gain, shift, scale):
    y = x * lax.rsqrt(jnp.mean(x * x, axis=-1, keepdims=True) + EPS)
    return (y * gain) * (1.0 + scale) + shift


def _mod_kernel(c_ref, w_ref, b_ref, o_ref):
    c = c_ref[...]
    a = c * (1.0 / (1.0 + jnp.exp(-c)))
    w = w_ref[0]
    a_hi = a.astype(BF16)
    a_lo = (a - a_hi.astype(F32)).astype(BF16)
    w_hi = w.astype(BF16)
    w_lo = (w - w_hi.astype(F32)).astype(BF16)
    acc = _dot(a_hi, w_hi) + _dot(a_hi, w_lo) + _dot(a_lo, w_hi)
    o_ref[0] = acc + b_ref[0]


def _modulation(c, w_mod, b_mod):
    depth, d, n = w_mod.shape
    b = c.shape[0]
    rows = -(-b // V7X_SUBLANES) * V7X_SUBLANES
    c_pad = jnp.pad(c, ((0, rows - b), (0, 0)))
    out = pl.pallas_call(
        _mod_kernel,
        grid=(depth, n // MOD_COLS),
        in_specs=[
            pl.BlockSpec((rows, d), lambda i, j: (0, 0)),
            pl.BlockSpec((1, d, MOD_COLS), lambda i, j: (i, 0, j)),
            pl.BlockSpec((1, 1, MOD_COLS), lambda i, j: (i, 0, j)),
        ],
        out_specs=pl.BlockSpec((1, rows, MOD_COLS), lambda i, j: (i, 0, j)),
        out_shape=jax.ShapeDtypeStruct((depth, rows, n), F32),
        compiler_params=_params(("parallel", "parallel")),
        name="modulation",
    )(c_pad, w_mod, b_mod.reshape(depth, 1, n))
    return out[:, :b]


def _pool_kernel(x_ref, halo_ref, gain_ref, sh_ref, sc_ref, gate_ref, w_ref, ps_ref, o_ref):
    t = pl.program_id(1)
    rows = x_ref.shape[1]
    group = w_ref.shape[1]
    x = x_ref[0]
    gain, shift, scale = gain_ref[...], sh_ref[0], sc_ref[0]
    h = _rms_modulate(x, gain, shift, scale)
    h_prev = _rms_modulate(halo_ref[0], gain, shift, scale)
    h_prev = jnp.where(t > 0, h_prev, 0.0)
    h_ext = jnp.concatenate([h_prev, h], axis=0)
    pos = t * rows + lax.broadcasted_iota(jnp.int32, (rows, 1), 0)
    ys = []
    for g, window in enumerate(POOL_WINDOWS):
        cols = slice(g * group, (g + 1) * group)
        s = h_ext[:, cols]
        step = 1
        while step < window:
            s = s + pltpu.roll(s, step, axis=0)
            step *= 2
        inv_cnt = 1.0 / jnp.minimum(pos + 1, window).astype(F32)
        pooled = s[POOL_HALO:] * inv_cnt - h[:, cols]
        ys.append(_dot(pooled.astype(BF16), w_ref[g]))
    y = jnp.concatenate(ys, axis=-1) * ps_ref[...]
    o_ref[0] = x + gate_ref[0] * y


def _pool_layer(x, gain, shift, scale, gate, w_pool, pool_scale):
    b, s, d = x.shape
    groups, group, _ = w_pool.shape
    halo_blocks = POOL_ROWS // POOL_HALO
    vec = pl.BlockSpec((1, 1, d), lambda i, t: (i, 0, 0))
    row = pl.BlockSpec((1, d), lambda i, t: (0, 0))
    return pl.pallas_call(
        _pool_kernel,
        grid=(b, s // POOL_ROWS),
        in_specs=[
            pl.BlockSpec((1, POOL_ROWS, d), lambda i, t: (i, t, 0)),
            pl.BlockSpec((1, POOL_HALO, d),
                         lambda i, t: (i, jnp.maximum(t * halo_blocks - 1, 0), 0)),
            row, vec, vec, vec,
            pl.BlockSpec((groups, group, group), lambda i, t: (0, 0, 0)),
            row,
        ],
        out_specs=pl.BlockSpec((1, POOL_ROWS, d), lambda i, t: (i, t, 0)),
        out_shape=jax.ShapeDtypeStruct((b, s, d), F32),
        compiler_params=_params(("parallel", "parallel")),
        name="pool_mixer",
    )(x, x, gain.reshape(1, d), shift.reshape(b, 1, d), scale.reshape(b, 1, d),
      gate.reshape(b, 1, d), w_pool.astype(BF16), pool_scale.reshape(1, d))


def _mlp_kernel(x_ref, gain_ref, sh_ref, sc_ref, gate_ref, wup_ref, wdn_ref, fin_ref,
                o_ref, h_scr, acc_scr, *, final_norm):
    x = x_ref[0]
    h_scr[...] = _rms_modulate(x, gain_ref[...], sh_ref[0], sc_ref[0]).astype(BF16)
    acc_scr[...] = jnp.zeros_like(acc_scr)
    n_chunks = wup_ref.shape[1] // MLP_FF_CHUNK

    def chunk(f, carry):
        f0 = pl.multiple_of(f * MLP_FF_CHUNK, MLP_FF_CHUNK)
        up = _dot(h_scr[...], wup_ref[:, pl.ds(f0, MLP_FF_CHUNK)])
        a = jnp.maximum(up, 0.0)
        acc_scr[...] += _dot((a * a).astype(BF16), wdn_ref[pl.ds(f0, MLP_FF_CHUNK), :])
        return carry

    lax.fori_loop(0, n_chunks, chunk, 0, unroll=True)
    out = x + gate_ref[0] * acc_scr[...]
    if final_norm:
        out = out * lax.rsqrt(jnp.mean(out * out, axis=-1, keepdims=True) + EPS)
        out = out * fin_ref[...]
    o_ref[0] = out


def _mlp_layer(x, gain, shift, scale, gate, w_up, w_down, norm_final, final_norm):
    b, s, d = x.shape
    ff = w_up.shape[1]
    vec = pl.BlockSpec((1, 1, d), lambda i, t: (i, 0, 0))
    row = pl.BlockSpec((1, d), lambda i, t: (0, 0))
    resident = dict(pipeline_mode=pl.Buffered(1))
    return pl.pallas_call(
        functools.partial(_mlp_kernel, final_norm=final_norm),
        grid=(b, s // MLP_ROWS),
        in_specs=[
            pl.BlockSpec((1, MLP_ROWS, d), lambda i, t: (i, t, 0)),
            row, vec, vec, vec,
            pl.BlockSpec((d, ff), lambda i, t: (0, 0), **resident),
            pl.BlockSpec((ff, d), lambda i, t: (0, 0), **resident),
            row,
        ],
        out_specs=pl.BlockSpec((1, MLP_ROWS, d), lambda i, t: (i, t, 0)),
        out_shape=jax.ShapeDtypeStruct((b, s, d), F32),
        scratch_shapes=[pltpu.VMEM((MLP_ROWS, d), BF16), pltpu.VMEM((MLP_ROWS, d), F32)],
        compiler_params=_params(("parallel", "parallel")),
        name="mlp_final" if final_norm else "mlp",
    )(x, gain.reshape(1, d), shift.reshape(b, 1, d), scale.reshape(b, 1, d),
      gate.reshape(b, 1, d), w_up.astype(BF16), w_down.astype(BF16), norm_final.reshape(1, d))


def _qkv_kernel(x_ref, gain_ref, sh_ref, sc_ref, wk_ref, wqvt_ref,
                qt_ref, k_ref, vt_ref, km_ref):
    t = pl.program_id(1)
    d = x_ref.shape[2]
    rows = x_ref.shape[1]
    blocks = rows // MOBA_BLOCK
    h = _rms_modulate(x_ref[0], gain_ref[...], sh_ref[0], sc_ref[0]).astype(BF16)
    k = _dot(h, wk_ref[...])
    k_ref[0] = k.astype(BF16)
    k_mean = jnp.mean(k.reshape(blocks, MOBA_BLOCK, d), axis=1)
    steps = km_ref.shape[1] // blocks
    for step in range(steps):
        @pl.when(t % steps == step)
        def _():
            km_ref[0, step * blocks:(step + 1) * blocks, :] = k_mean
    qvt = lax.dot_general(wqvt_ref[...], h, (((1,), (1,)), ((), ())), preferred_element_type=F32)
    qt_ref[0] = (qvt[:d] * (HEAD_DIM ** -0.5 * LOG2E)).astype(BF16)
    vt_ref[0] = qvt[d:].astype(BF16)


def _qkv_layer(x, gain, shift, scale, w_qkv):
    b, s, d = x.shape
    nb = s // MOBA_BLOCK
    km_rows = V7X_SUBLANES
    steps_per_km = km_rows * MOBA_BLOCK // QKV_ROWS
    vec = pl.BlockSpec((1, 1, d), lambda i, t: (i, 0, 0))
    row = pl.BlockSpec((1, d), lambda i, t: (0, 0))
    resident = dict(pipeline_mode=pl.Buffered(1))
    w_k = w_qkv[:, d:2 * d].astype(BF16)
    w_qvt = jnp.concatenate([w_qkv[:, :d], w_qkv[:, 2 * d:]], axis=1).T.astype(BF16)
    transposed = pl.BlockSpec((1, d, QKV_ROWS), lambda i, t: (i, 0, t))
    return pl.pallas_call(
        _qkv_kernel,
        grid=(b, s // QKV_ROWS),
        in_specs=[
            pl.BlockSpec((1, QKV_ROWS, d), lambda i, t: (i, t, 0)),
            row, vec, vec,
            pl.BlockSpec((d, d), lambda i, t: (0, 0), **resident),
            pl.BlockSpec((2 * d, d), lambda i, t: (0, 0), **resident),
        ],
        out_specs=[
            transposed,
            pl.BlockSpec((1, QKV_ROWS, d), lambda i, t: (i, t, 0)),
            transposed,
            pl.BlockSpec((1, km_rows, d), lambda i, t: (i, t // steps_per_km, 0)),
        ],
        out_shape=[
            jax.ShapeDtypeStruct((b, d, s), BF16),
            jax.ShapeDtypeStruct((b, s, d), BF16),
            jax.ShapeDtypeStruct((b, d, s), BF16),
            jax.ShapeDtypeStruct((b, nb, d), F32),
        ],
        compiler_params=_params(("parallel", "arbitrary")),
        name="qkv_proj",
    )(x, gain.reshape(1, d), shift.reshape(b, 1, d), scale.reshape(b, 1, d), w_k, w_qvt)


def _select_kernel(qt_ref, km_ref, o_ref):
    t = pl.program_id(2)
    nb = km_ref.shape[1]
    cols = qt_ref.shape[2]
    lane = lax.broadcasted_iota(jnp.int32, km_ref.shape[1:], 1)
    blk = lax.broadcasted_iota(jnp.int32, (nb, cols), 0)
    own_blk = (t * cols + lax.broadcasted_iota(jnp.int32, (nb, cols), 1)) // MOBA_BLOCK
    blk_f = blk.astype(F32)
    neg = -jnp.inf
    for hh in range(HEADS_PER_SLAB):
        in_head = (lane >= hh * HEAD_DIM) & (lane < (hh + 1) * HEAD_DIM)
        k_mean = jnp.where(in_head, km_ref[0], 0.0).astype(BF16)
        gate = jnp.where(blk < own_blk, _dot(k_mean, qt_ref[0]), neg)
        chosen = jnp.full((nb, cols), neg, F32)
        for _ in range(MOBA_TOPK):
            best = jnp.max(gate, axis=0, keepdims=True)
            first = jnp.min(jnp.where(gate == best, blk_f, float(nb)), axis=0, keepdims=True)
            hit = blk_f == first
            chosen = jnp.where(hit & (best > neg), 0.0, chosen)
            gate = jnp.where(hit, neg, gate)
        o_ref[0, hh] = chosen


def _select_blocks(q_t, k_mean):
    b, d, s = q_t.shape
    nb = k_mean.shape[1]
    slabs = d // V7X_LANES
    return pl.pallas_call(
        _select_kernel,
        grid=(b, slabs, s // SELECT_COLS),
        in_specs=[
            pl.BlockSpec((1, V7X_LANES, SELECT_COLS), lambda bi, sl, t: (bi, sl, t)),
            pl.BlockSpec((1, nb, V7X_LANES), lambda bi, sl, t: (bi, 0, sl)),
        ],
        out_specs=pl.BlockSpec((1, HEADS_PER_SLAB, nb, SELECT_COLS),
                               lambda bi, sl, t: (bi, sl, 0, t)),
        out_shape=jax.ShapeDtypeStruct((b, slabs * HEADS_PER_SLAB, nb, s), F32),
        compiler_params=_params(("parallel", "parallel", "parallel")),
        name="moba_select",
    )(q_t, k_mean)


def _bias_kernel(rb_ref, o_ref):
    head = pl.program_id(0)
    kk = lax.broadcasted_iota(jnp.int32, (MOBA_BLOCK, MOBA_BLOCK), 0)
    qq = lax.broadcasted_iota(jnp.int32, (MOBA_BLOCK, MOBA_BLOCK), 1)
    max_exact = NUM_BUCKETS // 2
    for off in range(BIAS_TILES):
        dist = MOBA_BLOCK * off + qq - kk
        dpos = jnp.maximum(dist, 0)
        nf = jnp.maximum(dpos, 1).astype(F32)
        large = max_exact + (jnp.log(nf / max_exact) / math.log(MAX_DISTANCE / max_exact)
                             * (NUM_BUCKETS - max_exact)).astype(jnp.int32)
        large = jnp.minimum(large, NUM_BUCKETS - 1)
        bucket = jnp.where(dpos < max_exact, dpos, large)
        val = jnp.zeros((MOBA_BLOCK, MOBA_BLOCK), F32)
        for bkt in range(NUM_BUCKETS):
            val = jnp.where(bucket == bkt, rb_ref[bkt, head], val)
        val = val * LOG2E
        if off == 0:
            val = jnp.where(dist >= 0, val, -jnp.inf)
        o_ref[0, off] = val


def _bias_tiles(rel_bias):
    heads = rel_bias.shape[1]
    return pl.pallas_call(
        _bias_kernel,
        grid=(heads,),
        in_specs=[pl.BlockSpec(memory_space=pltpu.SMEM)],
        out_specs=pl.BlockSpec((1, BIAS_TILES, MOBA_BLOCK, MOBA_BLOCK),
                               lambda h: (h, 0, 0, 0)),
        out_shape=jax.ShapeDtypeStruct((heads, BIAS_TILES, MOBA_BLOCK, MOBA_BLOCK), F32),
        compiler_params=_params(("parallel",)),
        name="t5_bias_tiles",
    )(rel_bias)


def _attn_kernel(far_ref, qt_ref, k_ref, vt_ref, sel_ref, bias_ref, o_ref):
    hq = pl.program_id(1)
    i = pl.program_id(2)
    neg = -jnp.inf
    own_group = i // KEY_GROUP
    heads = range(HEADS_PER_STEP)
    tiles = [(hh, n) for n in range(KEY_GROUP) for hh in heads]
    ones_rows = jnp.ones((DENOM_ROWS, MOBA_BLOCK), BF16)
    dim_row = lax.broadcasted_iota(jnp.int32, (V7X_LANES, MOBA_BLOCK), 0)

    def slab_of(hh):
        slab = hh // HEADS_PER_SLAB
        return slice(slab * V7X_LANES, (slab + 1) * V7X_LANES)

    q_heads = []
    for hh in heads:
        lo = (hh % HEADS_PER_SLAB) * HEAD_DIM
        q_slab = qt_ref[0, slab_of(hh), :]
        own = (dim_row >= lo) & (dim_row < lo + HEAD_DIM)
        q_heads.append(jnp.where(own, q_slab, jnp.zeros_like(q_slab)))

    def raw_scores(hh, j):
        k_blk = k_ref[0, pl.ds(pl.multiple_of(j * MOBA_BLOCK, MOBA_BLOCK), MOBA_BLOCK), slab_of(hh)]
        return _dot(k_blk, q_heads[hh])

    def value_rows(hh, j):
        v_t = vt_ref[0, hh * HEAD_DIM:(hh + 1) * HEAD_DIM,
                     pl.ds(pl.multiple_of(j * MOBA_BLOCK, MOBA_BLOCK), MOBA_BLOCK)]
        return jnp.concatenate([v_t, ones_rows], axis=0)

    def select_row(hh, j):
        off = i - j
        row = jnp.where(off > 0, sel_ref[0, hh, pl.ds(j, 1), :], neg)
        return jnp.where(off == 0, 0.0, row)

    def bias_tile(hh, j):
        return bias_ref[hh, jnp.clip(i - j, 0, NEAR_OFFSETS)]

    def far_bias(hh):
        return far_ref[hq * HEADS_PER_STEP + hh] * LOG2E

    def finish(accs):
        out_t = jnp.concatenate(
            [acc[:HEAD_DIM] * (1.0 / acc[HEAD_DIM:HEAD_DIM + 1]) for acc in accs], axis=0)
        return out_t.T.astype(BF16)

    refs = [jnp.max(raw_scores(hh, i) + bias_ref[hh, 0], axis=0, keepdims=True) for hh in heads]

    def one_pass(accs, group, shift):
        first_block = group * KEY_GROUP
        accs = list(accs)
        pending = []
        for t in range(len(tiles) + SCORE_LOOKAHEAD):
            if t < len(tiles):
                hh, n = tiles[t]
                pending.append(raw_scores(hh, first_block + n) + shift(hh, first_block + n))
            if t >= SCORE_LOOKAHEAD:
                hh, n = tiles[t - SCORE_LOOKAHEAD]
                p = jnp.exp2(pending[t - SCORE_LOOKAHEAD]).astype(BF16)
                accs[hh] = accs[hh] + _dot(value_rows(hh, first_block + n), p)
        return tuple(accs)

    def near_shift(hh, j):
        return bias_tile(hh, j) + (select_row(hh, j) - refs[hh])

    def far_shift(hh, j):
        return sel_ref[0, hh, pl.ds(j, 1), :] + (far_bias(hh) - refs[hh])

    zero = jnp.zeros((HEAD_DIM + DENOM_ROWS, MOBA_BLOCK), F32)
    accs = one_pass((zero,) * HEADS_PER_STEP, own_group, near_shift)
    accs = lax.cond(own_group > 0,
                    lambda a: one_pass(a, own_group - 1, near_shift), lambda a: a, accs)
    accs = lax.fori_loop(0, jnp.maximum(own_group - 1, 0),
                         lambda g, a: one_pass(a, g, far_shift), accs)

    def running_max_pass(_):
        def visit(t, states):
            j = i - t
            out = []
            for hh in heads:
                m, acc = states[hh]
                s = raw_scores(hh, j) + bias_tile(hh, j) + select_row(hh, j)
                m_new = jnp.maximum(m, jnp.max(s, axis=0, keepdims=True))
                p = jnp.exp2(s - m_new).astype(BF16)
                out.append((m_new, jnp.exp2(m - m_new) * acc + _dot(value_rows(hh, j), p)))
            return tuple(out)

        init = ((jnp.full((1, MOBA_BLOCK), neg, F32),
                 jnp.zeros((HEAD_DIM + DENOM_ROWS, MOBA_BLOCK), F32)),) * HEADS_PER_STEP
        states = lax.fori_loop(0, i + 1, visit, init)
        return finish([acc for (_, acc) in states])

    largest = functools.reduce(jnp.maximum, [jnp.max(jnp.abs(acc)) for acc in accs])
    overflowed = jnp.logical_not(largest < jnp.inf)
    o_ref[0] = lax.cond(overflowed, running_max_pass, lambda _: finish(accs), 0)


def _attention(q_t, k, v_t, select, bias_tiles, far_bias):
    b, d, s = q_t.shape
    nb = s // MOBA_BLOCK
    width = HEADS_PER_STEP * HEAD_DIM
    return pl.pallas_call(
        _attn_kernel,
        grid=(b, d // width, nb),
        in_specs=[
            pl.BlockSpec(memory_space=pltpu.SMEM),
            pl.BlockSpec((1, width, MOBA_BLOCK), lambda bi, hq, i: (bi, hq, i)),
            pl.BlockSpec((1, s, width), lambda bi, hq, i: (bi, 0, hq)),
            pl.BlockSpec((1, width, s), lambda bi, hq, i: (bi, hq, 0)),
            pl.BlockSpec((1, HEADS_PER_STEP, nb, MOBA_BLOCK), lambda bi, hq, i: (bi, hq, 0, i)),
            pl.BlockSpec((HEADS_PER_STEP, BIAS_TILES, MOBA_BLOCK, MOBA_BLOCK),
                         lambda bi, hq, i: (hq, 0, 0, 0)),
        ],
        out_specs=pl.BlockSpec((1, MOBA_BLOCK, width), lambda bi, hq, i: (bi, i, hq)),
        out_shape=jax.ShapeDtypeStruct((b, s, d), BF16),
        compiler_params=_params(("parallel", "parallel", "parallel")),
        name="moba_attention",
    )(far_bias, q_t, k, v_t, select, bias_tiles)


def _oproj_kernel(x_ref, o_ref_in, w_ref, gate_ref, out_ref):
    out_ref[0] = x_ref[0] + gate_ref[0] * _dot(o_ref_in[0], w_ref[...])


def _oproj_layer(x, o, w_o, gate):
    b, s, d = x.shape
    tile = pl.BlockSpec((1, MLP_ROWS, d), lambda i, t: (i, t, 0))
    return pl.pallas_call(
        _oproj_kernel,
        grid=(b, s // MLP_ROWS),
        in_specs=[
            tile, tile,
            pl.BlockSpec((d, d), lambda i, t: (0, 0), pipeline_mode=pl.Buffered(1)),
            pl.BlockSpec((1, 1, d), lambda i, t: (i, 0, 0)),
        ],
        out_specs=tile,
        out_shape=jax.ShapeDtypeStruct((b, s, d), F32),
        compiler_params=_params(("parallel", "parallel")),
        name="attn_out_proj",
    )(x, o, w_o.astype(BF16), gate.reshape(b, 1, d))


def kernel(x, c, rel_bias, w_mod, b_mod, norm_mix, norm_mlp, w_pool, pool_scale,
           w_qkv, w_o, w_up, w_down, norm_final):
    b, s, d = x.shape
    assert d % (HEADS_PER_STEP * HEAD_DIM) == 0 and s % MLP_ROWS == 0
    assert s % SELECT_COLS == 0 and (s // MOBA_BLOCK) % KEY_GROUP == 0
    assert w_mod.shape[0] == 2 and w_mod.shape[2] % MOD_COLS == 0
    mod = _modulation(c, w_mod, b_mod)
    sh1, sc1, g1, sh2, sc2, g2 = (mod[:, :, n * d:(n + 1) * d] for n in range(6))

    x = _pool_layer(x, norm_mix[0], sh1[0], sc1[0], g1[0], w_pool[0], pool_scale[0])
    x = _mlp_layer(x, norm_mlp[0], sh2[0], sc2[0], g2[0], w_up[0], w_down[0], norm_final, False)

    q_t, k, v_t, k_mean = _qkv_layer(x, norm_mix[1], sh1[1], sc1[1], w_qkv[0])
    select = _select_blocks(q_t, k_mean)
    o = _attention(q_t, k, v_t, select, _bias_tiles(rel_bias), rel_bias[NUM_BUCKETS - 1])
    x = _oproj_layer(x, o, w_o[0], g1[1])
    return _mlp_layer(x, norm_mlp[1], sh2[1], sc2[1], g2[1], w_up[1], w_down[1], norm_final, True)
```

```python
import functools
import math

import jax
import jax.numpy as jnp
from jax import lax
from jax.experimental import pallas as pl
from jax.experimental.pallas import tpu as pltpu

F32 = jnp.float32
BF16 = jnp.bfloat16

HEAD_DIM = 64
MOBA_BLOCK = 256
MOBA_TOPK = 3
NUM_BUCKETS = 32
MAX_DISTANCE = 1024
POOL_WINDOWS = (2, 4, 8, 16)
EPS = 1e-6

V7X_LANES = 128
V7X_SUBLANES = 8
V7X_VMEM_LIMIT_BYTES = 56 * 1024 * 1024

NEAR_OFFSETS = 5
KEY_GROUP = 4
DENOM_ROWS = 16
SCORE_LOOKAHEAD = 8
LOG2E = math.log2(math.e)
HEADS_PER_SLAB = V7X_LANES // HEAD_DIM
HEADS_PER_STEP = 4 * HEADS_PER_SLAB
POOL_HALO = max(POOL_WINDOWS)

POOL_ROWS = 512
MLP_ROWS = 512
MLP_FF_CHUNK = 512
QKV_ROWS = 512
SELECT_COLS = 2048
MOD_COLS = 1536


def _params(semantics):
    return pltpu.CompilerParams(dimension_semantics=semantics,
                                vmem_limit_bytes=V7X_VMEM_LIMIT_BYTES)


def _dot(a, b):
    return jnp.dot(a, b, preferred_element_type=F32)


def _rms_modulate(x, gain, shift, scale):
    y = x * lax.rsqrt(jnp.mean(x * x, axis=-1, keepdims=True) + EPS)
    return (y * gain) * (1.0 + scale) + shift


def _mod_kernel(c_ref, w_ref, b_ref, o_ref):
    c = c_ref[...]
    a = c * (1.0 / (1.0 + jnp.exp(-c)))
    w = w_ref[0]
    a_hi = a.astype(BF16)
    a_lo = (a - a_hi.astype(F32)).astype(BF16)
    w_hi = w.astype(BF16)
    w_lo = (w - w_hi.astype(F32)).astype(BF16)
    acc = _dot(a_hi, w_hi) + _dot(a_hi, w_lo) + _dot(a_lo, w_hi)
    o_ref[0] = acc + b_ref[0]


def _modulation(c, w_mod, b_mod):
    depth, d, n = w_mod.shape
    b = c.shape[0]
    rows = -(-b // V7X_SUBLANES) * V7X_SUBLANES
    c_pad = jnp.pad(c, ((0, rows - b), (0, 0)))
    out = pl.pallas_call(
        _mod_kernel,
        grid=(depth, n // MOD_COLS),
        in_specs=[
            pl.BlockSpec((rows, d), lambda i, j: (0, 0)),
            pl.BlockSpec((1, d, MOD_COLS), lambda i, j: (i, 0, j)),
            pl.BlockSpec((1, 1, MOD_COLS), lambda i, j: (i, 0, j)),
        ],
        out_specs=pl.BlockSpec((1, rows, MOD_COLS), lambda i, j: (i, 0, j)),
        out_shape=jax.ShapeDtypeStruct((depth, rows, n), F32),
        compiler_params=_params(("parallel", "parallel")),
        name="modulation",
    )(c_pad, w_mod, b_mod.reshape(depth, 1, n))
    return out[:, :b]


def _pool_kernel(x_ref, halo_ref, gain_ref, sh_ref, sc_ref, gate_ref, w_ref, ps_ref, o_ref):
    t = pl.program_id(1)
    rows = x_ref.shape[1]
    group = w_ref.shape[1]
    x = x_ref[0]
    gain, shift, scale = gain_ref[...], sh_ref[0], sc_ref[0]
    h = _rms_modulate(x, gain, shift, scale)
    h_prev = _rms_modulate(halo_ref[0], gain, shift, scale)
    h_prev = jnp.where(t > 0, h_prev, 0.0)
    h_ext = jnp.concatenate([h_prev, h], axis=0)
    pos = t * rows + lax.broadcasted_iota(jnp.int32, (rows, 1), 0)
    ys = []
    for g, window in enumerate(POOL_WINDOWS):
        cols = slice(g * group, (g + 1) * group)
        s = h_ext[:, cols]
        step = 1
        while step < window:
            s = s + pltpu.roll(s, step, axis=0)
            step *= 2
        inv_cnt = 1.0 / jnp.minimum(pos + 1, window).astype(F32)
        pooled = s[POOL_HALO:] * inv_cnt - h[:, cols]
        ys.append(_dot(pooled.astype(BF16), w_ref[g]))
    y = jnp.concatenate(ys, axis=-1) * ps_ref[...]
    o_ref[0] = x + gate_ref[0] * y


def _pool_layer(x, gain, shift, scale, gate, w_pool, pool_scale):
    b, s, d = x.shape
    groups, group, _ = w_pool.shape
    halo_blocks = POOL_ROWS // POOL_HALO
    vec = pl.BlockSpec((1, 1, d), lambda i, t: (i, 0, 0))
    row = pl.BlockSpec((1, d), lambda i, t: (0, 0))
    return pl.pallas_call(
        _pool_kernel,
        grid=(b, s // POOL_ROWS),
        in_specs=[
            pl.BlockSpec((1, POOL_ROWS, d), lambda i, t: (i, t, 0)),
            pl.BlockSpec((1, POOL_HALO, d),
                         lambda i, t: (i, jnp.maximum(t * halo_blocks - 1, 0), 0)),
            row, vec, vec, vec,
            pl.BlockSpec((groups, group, group), lambda i, t: (0, 0, 0)),
            row,
        ],
        out_specs=pl.BlockSpec((1, POOL_ROWS, d), lambda i, t: (i, t, 0)),
        out_shape=jax.ShapeDtypeStruct((b, s, d), F32),
        compiler_params=_params(("parallel", "parallel")),
        name="pool_mixer",
    )(x, x, gain.reshape(1, d), shift.reshape(b, 1, d), scale.reshape(b, 1, d),
      gate.reshape(b, 1, d), w_pool.astype(BF16), pool_scale.reshape(1, d))


def _mlp_kernel(x_ref, gain_ref, sh_ref, sc_ref, gate_ref, wup_ref, wdn_ref, fin_ref,
                o_ref, h_scr, acc_scr, *, final_norm):
    x = x_ref[0]
    h_scr[...] = _rms_modulate(x, gain_ref[...], sh_ref[0], sc_ref[0]).astype(BF16)
    acc_scr[...] = jnp.zeros_like(acc_scr)
    n_chunks = wup_ref.shape[1] // MLP_FF_CHUNK

    def chunk(f, carry):
        f0 = pl.multiple_of(f * MLP_FF_CHUNK, MLP_FF_CHUNK)
        up = _dot(h_scr[...], wup_ref[:, pl.ds(f0, MLP_FF_CHUNK)])
        a = jnp.maximum(up, 0.0)
        acc_scr[...] += _dot((a * a).astype(BF16), wdn_ref[pl.ds(f0, MLP_FF_CHUNK), :])
        return carry

    lax.fori_loop(0, n_chunks, chunk, 0, unroll=True)
    out = x + gate_ref[0] * acc_scr[...]
    if final_norm:
        out = out * lax.rsqrt(jnp.mean(out * out, axis=-1, keepdims=True) + EPS)
        out = out * fin_ref[...]
    o_ref[0] = out


def _mlp_layer(x, gain, shift, scale, gate, w_up, w_down, norm_final, final_norm):
    b, s, d = x.shape
    ff = w_up.shape[1]
    vec = pl.BlockSpec((1, 1, d), lambda i, t: (i, 0, 0))
    row = pl.BlockSpec((1, d), lambda i, t: (0, 0))
    resident = dict(pipeline_mode=pl.Buffered(1))
    return pl.pallas_call(
        functools.partial(_mlp_kernel, final_norm=final_norm),
        grid=(b, s // MLP_ROWS),
        in_specs=[
            pl.BlockSpec((1, MLP_ROWS, d), lambda i, t: (i, t, 0)),
            row, vec, vec, vec,
            pl.BlockSpec((d, ff), lambda i, t: (0, 0), **resident),
            pl.BlockSpec((ff, d), lambda i, t: (0, 0), **resident),
            row,
        ],
        out_specs=pl.BlockSpec((1, MLP_ROWS, d), lambda i, t: (i, t, 0)),
        out_shape=jax.ShapeDtypeStruct((b, s, d), F32),
        scratch_shapes=[pltpu.VMEM((MLP_ROWS, d), BF16), pltpu.VMEM((MLP_ROWS, d), F32)],
        compiler_params=_params(("parallel", "parallel")),
        name="mlp_final" if final_norm else "mlp",
    )(x, gain.reshape(1, d), shift.reshape(b, 1, d), scale.reshape(b, 1, d),
      gate.reshape(b, 1, d), w_up.astype(BF16), w_down.astype(BF16), norm_final.reshape(1, d))


def _qkv_kernel(x_ref, gain_ref, sh_ref, sc_ref, wk_ref, wqvt_ref,
                qt_ref, k_ref, vt_ref, km_ref):
    t = pl.program_id(1)
    d = x_ref.shape[2]
    rows = x_ref.shape[1]
    blocks = rows // MOBA_BLOCK
    h = _rms_modulate(x_ref[0], gain_ref[...], sh_ref[0], sc_ref[0]).astype(BF16)
    k = _dot(h, wk_ref[...])
    k_ref[0] = k.astype(BF16)
    k_mean = jnp.mean(k.reshape(blocks, MOBA_BLOCK, d), axis=1)
    steps = km_ref.shape[1] // blocks
    for step in range(steps):
        @pl.when(t % steps == step)
        def _():
            km_ref[0, step * blocks:(step + 1) * blocks, :] = k_mean
    qvt = lax.dot_general(wqvt_ref[...], h, (((1,), (1,)), ((), ())), preferred_element_type=F32)
    qt_ref[0] = (qvt[:d] * (HEAD_DIM ** -0.5 * LOG2E)).astype(BF16)
    vt_ref[0] = qvt[d:].astype(BF16)


def _qkv_layer(x, gain, shift, scale, w_qkv):
    b, s, d = x.shape
    nb = s // MOBA_BLOCK
    km_rows = V7X_SUBLANES
    steps_per_km = km_rows * MOBA_BLOCK // QKV_ROWS
    vec = pl.BlockSpec((1, 1, d), lambda i, t: (i, 0, 0))
    row = pl.BlockSpec((1, d), lambda i, t: (0, 0))
    resident = dict(pipeline_mode=pl.Buffered(1))
    w_k = w_qkv[:, d:2 * d].astype(BF16)
    w_qvt = jnp.concatenate([w_qkv[:, :d], w_qkv[:, 2 * d:]], axis=1).T.astype(BF16)
    transposed = pl.BlockSpec((1, d, QKV_ROWS), lambda i, t: (i, 0, t))
    return pl.pallas_call(
        _qkv_kernel,
        grid=(b, s // QKV_ROWS),
        in_specs=[
            pl.BlockSpec((1, QKV_ROWS, d), lambda i, t: (i, t, 0)),
            row, vec, vec,
            pl.BlockSpec((d, d), lambda i, t: (0, 0), **resident),
            pl.BlockSpec((2 * d, d), lambda i, t: (0, 0), **resident),
        ],
        out_specs=[
            transposed,
            pl.BlockSpec((1, QKV_ROWS, d), lambda i, t: (i, t, 0)),
            transposed,
            pl.BlockSpec((1, km_rows, d), lambda i, t: (i, t // steps_per_km, 0)),
        ],
        out_shape=[
            jax.ShapeDtypeStruct((b, d, s), BF16),
            jax.ShapeDtypeStruct((b, s, d), BF16),
            jax.ShapeDtypeStruct((b, d, s), BF16),
            jax.ShapeDtypeStruct((b, nb, d), F32),
        ],
        compiler_params=_params(("parallel", "arbitrary")),
        name="qkv_proj",
    )(x, gain.reshape(1, d), shift.reshape(b, 1, d), scale.reshape(b, 1, d), w_k, w_qvt)


def _select_kernel(qt_ref, km_ref, o_ref):
    t = pl.program_id(2)
    nb = km_ref.shape[1]
    cols = qt_ref.shape[2]
    lane = lax.broadcasted_iota(jnp.int32, km_ref.shape[1:], 1)
    blk = lax.broadcasted_iota(jnp.int32, (nb, cols), 0)
    own_blk = (t * cols + lax.broadcasted_iota(jnp.int32, (nb, cols), 1)) // MOBA_BLOCK
    blk_f = blk.astype(F32)
    neg = -jnp.inf
    for hh in range(HEADS_PER_SLAB):
        in_head = (lane >= hh * HEAD_DIM) & (lane < (hh + 1) * HEAD_DIM)
        k_mean = jnp.where(in_head, km_ref[0], 0.0).astype(BF16)
        gate = jnp.where(blk < own_blk, _dot(k_mean, qt_ref[0]), neg)
        chosen = jnp.full((nb, cols), neg, F32)
        for _ in range(MOBA_TOPK):
            best = jnp.max(gate, axis=0, keepdims=True)
            first = jnp.min(jnp.where(gate == best, blk_f, float(nb)), axis=0, keepdims=True)
            hit = blk_f == first
            chosen = jnp.where(hit & (best > neg), 0.0, chosen)
            gate = jnp.where(hit, neg, gate)
        o_ref[0, hh] = chosen


def _select_blocks(q_t, k_mean):
    b, d, s = q_t.shape
    nb = k_mean.shape[1]
    slabs = d // V7X_LANES
    return pl.pallas_call(
        _select_kernel,
        grid=(b, slabs, s // SELECT_COLS),
        in_specs=[
            pl.BlockSpec((1, V7X_LANES, SELECT_COLS), lambda bi, sl, t: (bi, sl, t)),
            pl.BlockSpec((1, nb, V7X_LANES), lambda bi, sl, t: (bi, 0, sl)),
        ],
        out_specs=pl.BlockSpec((1, HEADS_PER_SLAB, nb, SELECT_COLS),
                               lambda bi, sl, t: (bi, sl, 0, t)),
        out_shape=jax.ShapeDtypeStruct((b, slabs * HEADS_PER_SLAB, nb, s), F32),
        compiler_params=_params(("parallel", "parallel", "parallel")),
        name="moba_select",
    )(q_t, k_mean)


def _bias_kernel(rb_ref, o_ref):
    head = pl.program_id(0)
    kk = lax.broadcasted_iota(jnp.int32, (MOBA_BLOCK, MOBA_BLOCK), 0)
    qq = lax.broadcasted_iota(jnp.int32, (MOBA_BLOCK, MOBA_BLOCK), 1)
    max_exact = NUM_BUCKETS // 2
    for off in range(NEAR_OFFSETS):
        dist = MOBA_BLOCK * off + qq - kk
        dpos = jnp.maximum(dist, 0)
        nf = jnp.maximum(dpos, 1).astype(F32)
        large = max_exact + (jnp.log(nf / max_exact) / math.log(MAX_DISTANCE / max_exact)
                             * (NUM_BUCKETS - max_exact)).astype(jnp.int32)
        large = jnp.minimum(large, NUM_BUCKETS - 1)
        bucket = jnp.where(dpos < max_exact, dpos, large)
        val = jnp.zeros((MOBA_BLOCK, MOBA_BLOCK), F32)
        for bkt in range(NUM_BUCKETS):
            val = jnp.where(bucket == bkt, rb_ref[bkt, head], val)
        val = val * LOG2E
        if off == 0:
            val = jnp.where(dist >= 0, val, -jnp.inf)
        o_ref[0, off] = val


def _bias_tiles(rel_bias):
    heads = rel_bias.shape[1]
    return pl.pallas_call(
        _bias_kernel,
        grid=(heads,),
        in_specs=[pl.BlockSpec(memory_space=pltpu.SMEM)],
        out_specs=pl.BlockSpec((1, NEAR_OFFSETS, MOBA_BLOCK, MOBA_BLOCK),
                               lambda h: (h, 0, 0, 0)),
        out_shape=jax.ShapeDtypeStruct((heads, NEAR_OFFSETS, MOBA_BLOCK, MOBA_BLOCK), F32),
        compiler_params=_params(("parallel",)),
        name="t5_bias_tiles",
    )(rel_bias)


def _attn_kernel(far_ref, qt_ref, k_ref, vt_ref, sel_ref, bias_ref, o_ref):
    hq = pl.program_id(1)
    i = pl.program_id(2)
    neg = -jnp.inf
    heads = range(HEADS_PER_STEP)
    tiles = [(hh, n) for n in range(KEY_GROUP) for hh in heads]
    ones_rows = jnp.ones((DENOM_ROWS, MOBA_BLOCK), BF16)
    dim_row = lax.broadcasted_iota(jnp.int32, (V7X_LANES, MOBA_BLOCK), 0)

    def slab_of(hh):
        slab = hh // HEADS_PER_SLAB
        return slice(slab * V7X_LANES, (slab + 1) * V7X_LANES)

    q_heads = []
    for hh in heads:
        lo = (hh % HEADS_PER_SLAB) * HEAD_DIM
        q_slab = qt_ref[0, slab_of(hh), :]
        own = (dim_row >= lo) & (dim_row < lo + HEAD_DIM)
        q_heads.append(jnp.where(own, q_slab, jnp.zeros_like(q_slab)))

    def raw_scores(hh, j):
        k_blk = k_ref[0, pl.ds(pl.multiple_of(j * MOBA_BLOCK, MOBA_BLOCK), MOBA_BLOCK), slab_of(hh)]
        return _dot(k_blk, q_heads[hh])

    def value_rows(hh, j):
        v_t = vt_ref[0, hh * HEAD_DIM:(hh + 1) * HEAD_DIM,
                     pl.ds(pl.multiple_of(j * MOBA_BLOCK, MOBA_BLOCK), MOBA_BLOCK)]
        return jnp.concatenate([v_t, ones_rows], axis=0)

    def select_row(hh, j):
        off = i - j
        row = jnp.where(off > 0, sel_ref[0, hh, pl.ds(j, 1), :], neg)
        return jnp.where(off == 0, 0.0, row)

    def far_bias(hh):
        return far_ref[hq * HEADS_PER_STEP + hh] * LOG2E

    def bias_tile(hh, j):
        off = i - j
        tile = bias_ref[hh, jnp.clip(off, 0, NEAR_OFFSETS - 1)]
        return jnp.where(off >= NEAR_OFFSETS, far_bias(hh), tile)

    def finish(accs):
        out_t = jnp.concatenate(
            [acc[:HEAD_DIM] * (1.0 / acc[HEAD_DIM:HEAD_DIM + 1]) for acc in accs], axis=0)
        return out_t.T.astype(BF16)

    own = [raw_scores(hh, i) + bias_ref[hh, 0] for hh in heads]
    refs = [jnp.max(s, axis=0, keepdims=True) for s in own]

    def one_pass(accs, block_of, shift, ready=None):
        accs = list(accs)
        pending = []
        for t in range(len(tiles) + SCORE_LOOKAHEAD):
            if t < len(tiles):
                hh, n = tiles[t]
                if ready is not None and n == 0:
                    pending.append(ready[hh])
                else:
                    pending.append(raw_scores(hh, block_of(n)) + shift(hh, n))
            if t >= SCORE_LOOKAHEAD:
                hh, n = tiles[t - SCORE_LOOKAHEAD]
                p = jnp.exp2(pending[t - SCORE_LOOKAHEAD]).astype(BF16)
                accs[hh] = accs[hh] + _dot(value_rows(hh, block_of(n)), p)
        return tuple(accs)

    def near_group(accs, first_off, ready=None):
        def block_of(n):
            return jnp.maximum(i - first_off - n, 0)

        def shift(hh, n):
            j = i - first_off - n
            row = jnp.where(j >= 0, sel_ref[0, hh, pl.ds(block_of(n), 1), :], neg)
            if first_off + n >= NEAR_OFFSETS:
                return row + (far_bias(hh) - refs[hh])
            return bias_ref[hh, first_off + n] + (row - refs[hh])

        return one_pass(accs, block_of, shift, ready)

    def far_group(g, accs):
        def block_of(n):
            return g * KEY_GROUP + n

        def shift(hh, n):
            row = jnp.where(block_of(n) <= i - 2 * KEY_GROUP,
                            sel_ref[0, hh, pl.ds(block_of(n), 1), :], neg)
            return row + (far_bias(hh) - refs[hh])

        return one_pass(accs, block_of, shift)

    zero = jnp.zeros((HEAD_DIM + DENOM_ROWS, MOBA_BLOCK), F32)
    accs = near_group((zero,) * HEADS_PER_STEP, 0, [own[hh] - refs[hh] for hh in heads])
    accs = lax.cond(i >= KEY_GROUP, lambda a: near_group(a, KEY_GROUP), lambda a: a, accs)
    n_far_blocks = jnp.maximum(i - (2 * KEY_GROUP - 1), 0)
    accs = lax.fori_loop(0, (n_far_blocks + KEY_GROUP - 1) // KEY_GROUP, far_group, accs)

    def running_max_pass():
        def visit(t, states):
            j = i - t
            out = []
            for hh in heads:
                m, acc = states[hh]
                s = raw_scores(hh, j) + bias_tile(hh, j) + select_row(hh, j)
                m_new = jnp.maximum(m, jnp.max(s, axis=0, keepdims=True))
                p = jnp.exp2(s - m_new).astype(BF16)
                out.append((m_new, jnp.exp2(m - m_new) * acc + _dot(value_rows(hh, j), p)))
            return tuple(out)

        init = ((jnp.full((1, MOBA_BLOCK), neg, F32),
                 jnp.zeros((HEAD_DIM + DENOM_ROWS, MOBA_BLOCK), F32)),) * HEADS_PER_STEP
        states = lax.fori_loop(0, i + 1, visit, init)
        return finish([acc for (_, acc) in states])

    largest = functools.reduce(jnp.maximum, [jnp.max(jnp.abs(acc)) for acc in accs])
    overflowed = jnp.logical_not(largest < jnp.inf)
    o_ref[0] = finish(accs)

    @pl.when(overflowed)
    def _():
        o_ref[0] = running_max_pass()


def _attention(q_t, k, v_t, select, bias_tiles, far_bias):
    b, d, s = q_t.shape
    nb = s // MOBA_BLOCK
    width = HEADS_PER_STEP * HEAD_DIM
    return pl.pallas_call(
        _attn_kernel,
        grid=(b, d // width, nb),
        in_specs=[
            pl.BlockSpec(memory_space=pltpu.SMEM),
            pl.BlockSpec((1, width, MOBA_BLOCK), lambda bi, hq, i: (bi, hq, i)),
            pl.BlockSpec((1, s, width), lambda bi, hq, i: (bi, 0, hq)),
            pl.BlockSpec((1, width, s), lambda bi, hq, i: (bi, hq, 0)),
            pl.BlockSpec((1, HEADS_PER_STEP, nb, MOBA_BLOCK), lambda bi, hq, i: (bi, hq, 0, i)),
            pl.BlockSpec((HEADS_PER_STEP, NEAR_OFFSETS, MOBA_BLOCK, MOBA_BLOCK),
                         lambda bi, hq, i: (hq, 0, 0, 0), pipeline_mode=pl.Buffered(1)),
        ],
        out_specs=pl.BlockSpec((1, MOBA_BLOCK, width), lambda bi, hq, i: (bi, i, hq)),
        out_shape=jax.ShapeDtypeStruct((b, s, d), BF16),
        compiler_params=_params(("parallel", "parallel", "parallel")),
        name="moba_attention",
    )(far_bias, q_t, k, v_t, select, bias_tiles)


def _oproj_kernel(x_ref, o_ref_in, w_ref, gate_ref, out_ref):
    out_ref[0] = x_ref[0] + gate_ref[0] * _dot(o_ref_in[0], w_ref[...])


def _oproj_layer(x, o, w_o, gate):
    b, s, d = x.shape
    tile = pl.BlockSpec((1, MLP_ROWS, d), lambda i, t: (i, t, 0))
    return pl.pallas_call(
        _oproj_kernel,
        grid=(b, s // MLP_ROWS),
        in_specs=[
            tile, tile,
            pl.BlockSpec((d, d), lambda i, t: (0, 0), pipeline_mode=pl.Buffered(1)),
            pl.BlockSpec((1, 1, d), lambda i, t: (i, 0, 0)),
        ],
        out_specs=tile,
        out_shape=jax.ShapeDtypeStruct((b, s, d), F32),
        compiler_params=_params(("parallel", "parallel")),
        name="attn_out_proj",
    )(x, o, w_o.astype(BF16), gate.reshape(b, 1, d))


def kernel(x, c, rel_bias, w_mod, b_mod, norm_mix, norm_mlp, w_pool, pool_scale,
           w_qkv, w_o, w_up, w_down, norm_final):
    b, s, d = x.shape
    assert d % (HEADS_PER_STEP * HEAD_DIM) == 0 and s % MLP_ROWS == 0
    assert s % SELECT_COLS == 0 and (s // MOBA_BLOCK) % KEY_GROUP == 0
    assert w_mod.shape[0] == 2 and w_mod.shape[2] % MOD_COLS == 0
    mod = _modulation(c, w_mod, b_mod)
    sh1, sc1, g1, sh2, sc2, g2 = (mod[:, :, n * d:(n + 1) * d] for n in range(6))

    x = _pool_layer(x, norm_mix[0], sh1[0], sc1[0], g1[0], w_pool[0], pool_scale[0])
    x = _mlp_layer(x, norm_mlp[0], sh2[0], sc2[0], g2[0], w_up[0], w_down[0], norm_final, False)

    q_t, k, v_t, k_mean = _qkv_layer(x, norm_mix[1], sh1[1], sc1[1], w_qkv[0])
    select = _select_blocks(q_t, k_mean)
    o = _attention(q_t, k, v_t, select, _bias_tiles(rel_bias), rel_bias[NUM_BUCKETS - 1])
    x = _oproj_layer(x, o, w_o[0], g1[1])
    return _mlp_layer(x, norm_mlp[1], sh2[1], sc2[1], g2[1], w_up[1], w_down[1], norm_final, True)
```

```python
import functools
import math

import jax
import jax.numpy as jnp
from jax import lax
from jax.experimental import pallas as pl
from jax.experimental.pallas import tpu as pltpu

F32 = jnp.float32
BF16 = jnp.bfloat16

HEAD_DIM = 64
MOBA_BLOCK = 256
MOBA_TOPK = 3
NUM_BUCKETS = 32
MAX_DISTANCE = 1024
POOL_WINDOWS = (2, 4, 8, 16)
EPS = 1e-6

V7X_LANES = 128
V7X_SUBLANES = 8
V7X_VMEM_LIMIT_BYTES = 56 * 1024 * 1024

NEAR_OFFSETS = 5
KEY_GROUP = 4
DENOM_ROWS = 16
SCORE_LOOKAHEAD = 8
LOG2E = math.log2(math.e)
HEADS_PER_SLAB = V7X_LANES // HEAD_DIM
HEADS_PER_STEP = 4 * HEADS_PER_SLAB
POOL_HALO = max(POOL_WINDOWS)

POOL_ROWS = 512
MLP_ROWS = 512
MLP_FF_CHUNK = 512
QKV_ROWS = 512
MOD_COLS = 1536


def _params(semantics):
    return pltpu.CompilerParams(dimension_semantics=semantics,
                                vmem_limit_bytes=V7X_VMEM_LIMIT_BYTES)


def _dot(a, b):
    return jnp.dot(a, b, preferred_element_type=F32)


def _rms_modulate(x, gain, shift, scale):
    y = x * lax.rsqrt(jnp.mean(x * x, axis=-1, keepdims=True) + EPS)
    return (y * gain) * (1.0 + scale) + shift


def _mod_kernel(c_ref, w_ref, b_ref, o_ref):
    c = c_ref[...]
    a = c * (1.0 / (1.0 + jnp.exp(-c)))
    w = w_ref[0]
    a_hi = a.astype(BF16)
    a_lo = (a - a_hi.astype(F32)).astype(BF16)
    w_hi = w.astype(BF16)
    w_lo = (w - w_hi.astype(F32)).astype(BF16)
    acc = _dot(a_hi, w_hi) + _dot(a_hi, w_lo) + _dot(a_lo, w_hi)
    o_ref[0] = acc + b_ref[0]


def _modulation(c, w_mod, b_mod):
    depth, d, n = w_mod.shape
    b = c.shape[0]
    rows = -(-b // V7X_SUBLANES) * V7X_SUBLANES
    c_pad = jnp.pad(c, ((0, rows - b), (0, 0)))
    out = pl.pallas_call(
        _mod_kernel,
        grid=(depth, n // MOD_COLS),
        in_specs=[
            pl.BlockSpec((rows, d), lambda i, j: (0, 0)),
            pl.BlockSpec((1, d, MOD_COLS), lambda i, j: (i, 0, j)),
            pl.BlockSpec((1, 1, MOD_COLS), lambda i, j: (i, 0, j)),
        ],
        out_specs=pl.BlockSpec((1, rows, MOD_COLS), lambda i, j: (i, 0, j)),
        out_shape=jax.ShapeDtypeStruct((depth, rows, n), F32),
        compiler_params=_params(("parallel", "parallel")),
        name="modulation",
    )(c_pad, w_mod, b_mod.reshape(depth, 1, n))
    return out[:, :b]


def _pool_kernel(x_ref, halo_ref, gain_ref, sh_ref, sc_ref, gate_ref, w_ref, ps_ref, o_ref):
    t = pl.program_id(1)
    rows = x_ref.shape[1]
    group = w_ref.shape[1]
    x = x_ref[0]
    gain, shift, scale = gain_ref[...], sh_ref[0], sc_ref[0]
    h = _rms_modulate(x, gain, shift, scale)
    h_prev = _rms_modulate(halo_ref[0], gain, shift, scale)
    h_prev = jnp.where(t > 0, h_prev, 0.0)
    h_ext = jnp.concatenate([h_prev, h], axis=0)
    pos = t * rows + lax.broadcasted_iota(jnp.int32, (rows, 1), 0)
    ys = []
    for g, window in enumerate(POOL_WINDOWS):
        cols = slice(g * group, (g + 1) * group)
        s = h_ext[:, cols]
        step = 1
        while step < window:
            s = s + pltpu.roll(s, step, axis=0)
            step *= 2
        inv_cnt = 1.0 / jnp.minimum(pos + 1, window).astype(F32)
        pooled = s[POOL_HALO:] * inv_cnt - h[:, cols]
        ys.append(_dot(pooled.astype(BF16), w_ref[g]))
    y = jnp.concatenate(ys, axis=-1) * ps_ref[...]
    o_ref[0] = x + gate_ref[0] * y


def _pool_layer(x, gain, shift, scale, gate, w_pool, pool_scale):
    b, s, d = x.shape
    groups, group, _ = w_pool.shape
    halo_blocks = POOL_ROWS // POOL_HALO
    vec = pl.BlockSpec((1, 1, d), lambda i, t: (i, 0, 0))
    row = pl.BlockSpec((1, d), lambda i, t: (0, 0))
    return pl.pallas_call(
        _pool_kernel,
        grid=(b, s // POOL_ROWS),
        in_specs=[
            pl.BlockSpec((1, POOL_ROWS, d), lambda i, t: (i, t, 0)),
            pl.BlockSpec((1, POOL_HALO, d),
                         lambda i, t: (i, jnp.maximum(t * halo_blocks - 1, 0), 0)),
            row, vec, vec, vec,
            pl.BlockSpec((groups, group, group), lambda i, t: (0, 0, 0)),
            row,
        ],
        out_specs=pl.BlockSpec((1, POOL_ROWS, d), lambda i, t: (i, t, 0)),
        out_shape=jax.ShapeDtypeStruct((b, s, d), F32),
        compiler_params=_params(("parallel", "parallel")),
        name="pool_mixer",
    )(x, x, gain.reshape(1, d), shift.reshape(b, 1, d), scale.reshape(b, 1, d),
      gate.reshape(b, 1, d), w_pool.astype(BF16), pool_scale.reshape(1, d))


def _mlp_kernel(x_ref, gain_ref, sh_ref, sc_ref, gate_ref, wup_ref, wdn_ref, fin_ref,
                o_ref, h_scr, acc_scr, *, final_norm):
    x = x_ref[0]
    h_scr[...] = _rms_modulate(x, gain_ref[...], sh_ref[0], sc_ref[0]).astype(BF16)
    acc_scr[...] = jnp.zeros_like(acc_scr)
    n_chunks = wup_ref.shape[1] // MLP_FF_CHUNK

    def chunk(f, carry):
        f0 = pl.multiple_of(f * MLP_FF_CHUNK, MLP_FF_CHUNK)
        up = _dot(h_scr[...], wup_ref[:, pl.ds(f0, MLP_FF_CHUNK)])
        a = jnp.maximum(up, 0.0)
        acc_scr[...] += _dot((a * a).astype(BF16), wdn_ref[pl.ds(f0, MLP_FF_CHUNK), :])
        return carry

    lax.fori_loop(0, n_chunks, chunk, 0, unroll=True)
    out = x + gate_ref[0] * acc_scr[...]
    if final_norm:
        out = out * lax.rsqrt(jnp.mean(out * out, axis=-1, keepdims=True) + EPS)
        out = out * fin_ref[...]
    o_ref[0] = out


def _mlp_layer(x, gain, shift, scale, gate, w_up, w_down, norm_final, final_norm):
    b, s, d = x.shape
    ff = w_up.shape[1]
    vec = pl.BlockSpec((1, 1, d), lambda i, t: (i, 0, 0))
    row = pl.BlockSpec((1, d), lambda i, t: (0, 0))
    resident = dict(pipeline_mode=pl.Buffered(1))
    return pl.pallas_call(
        functools.partial(_mlp_kernel, final_norm=final_norm),
        grid=(b, s // MLP_ROWS),
        in_specs=[
            pl.BlockSpec((1, MLP_ROWS, d), lambda i, t: (i, t, 0)),
            row, vec, vec, vec,
            pl.BlockSpec((d, ff), lambda i, t: (0, 0), **resident),
            pl.BlockSpec((ff, d), lambda i, t: (0, 0), **resident),
            row,
        ],
        out_specs=pl.BlockSpec((1, MLP_ROWS, d), lambda i, t: (i, t, 0)),
        out_shape=jax.ShapeDtypeStruct((b, s, d), F32),
        scratch_shapes=[pltpu.VMEM((MLP_ROWS, d), BF16), pltpu.VMEM((MLP_ROWS, d), F32)],
        compiler_params=_params(("parallel", "parallel")),
        name="mlp_final" if final_norm else "mlp",
    )(x, gain.reshape(1, d), shift.reshape(b, 1, d), scale.reshape(b, 1, d),
      gate.reshape(b, 1, d), w_up.astype(BF16), w_down.astype(BF16), norm_final.reshape(1, d))


def _qkv_kernel(x_ref, gain_ref, sh_ref, sc_ref, wk_ref, wqvt_ref,
                qt_ref, k_ref, vt_ref, sel_ref, km_ref):
    t = pl.program_id(1)
    d = x_ref.shape[2]
    rows = x_ref.shape[1]
    blocks = rows // MOBA_BLOCK
    n_blocks = km_ref.shape[0]
    neg = -jnp.inf
    h = _rms_modulate(x_ref[0], gain_ref[...], sh_ref[0], sc_ref[0]).astype(BF16)
    k = _dot(h, wk_ref[...])
    k_ref[0] = k.astype(BF16)
    k_mean = jnp.mean(k.reshape(blocks, MOBA_BLOCK, d), axis=1)

    @pl.when(t == 0)
    def _():
        km_ref[...] = jnp.zeros_like(km_ref)

    for step in range(n_blocks // blocks):
        @pl.when(t == step)
        def _():
            km_ref[step * blocks:(step + 1) * blocks, :] = k_mean

    lane = lax.broadcasted_iota(jnp.int32, (n_blocks, V7X_LANES), 1)
    blk = lax.broadcasted_iota(jnp.int32, (n_blocks, rows), 0)
    own_blk = (t * rows + lax.broadcasted_iota(jnp.int32, (n_blocks, rows), 1)) // MOBA_BLOCK
    blk_f = blk.astype(F32)

    def transposed(w_rows):
        return lax.dot_general(wqvt_ref[w_rows, :], h, (((1,), (1,)), ((), ())),
                               preferred_element_type=F32)

    q_t = (transposed(slice(0, d)) * (HEAD_DIM ** -0.5 * LOG2E)).astype(BF16)
    qt_ref[0] = q_t
    gates = []
    for head in range(d // HEAD_DIM):
        slab = slice(head // HEADS_PER_SLAB * V7X_LANES, (head // HEADS_PER_SLAB + 1) * V7X_LANES)
        lo = (head % HEADS_PER_SLAB) * HEAD_DIM
        in_head = (lane >= lo) & (lane < lo + HEAD_DIM)
        means = jnp.where(in_head, km_ref[:, slab], 0.0).astype(BF16)
        gates.append(jnp.where(blk < own_blk, _dot(means, q_t[slab]), neg))
    vt_ref[0] = transposed(slice(d, 2 * d)).astype(BF16)
    for head, gate in enumerate(gates):
        chosen = jnp.full((n_blocks, rows), neg, F32)
        for _ in range(MOBA_TOPK):
            best = jnp.max(gate, axis=0, keepdims=True)
            first = jnp.min(jnp.where(gate == best, blk_f, float(n_blocks)), axis=0, keepdims=True)
            hit = blk_f == first
            chosen = jnp.where(hit & (best > neg), 0.0, chosen)
            gate = jnp.where(hit, neg, gate)
        sel_ref[0, head] = chosen


def _qkv_layer(x, gain, shift, scale, w_qkv):
    b, s, d = x.shape
    nb = s // MOBA_BLOCK
    heads = d // HEAD_DIM
    vec = pl.BlockSpec((1, 1, d), lambda i, t: (i, 0, 0))
    row = pl.BlockSpec((1, d), lambda i, t: (0, 0))
    resident = dict(pipeline_mode=pl.Buffered(1))
    w_k = w_qkv[:, d:2 * d].astype(BF16)
    w_qvt = jnp.concatenate([w_qkv[:, :d], w_qkv[:, 2 * d:]], axis=1).T.astype(BF16)
    transposed = pl.BlockSpec((1, d, QKV_ROWS), lambda i, t: (i, 0, t))
    return pl.pallas_call(
        _qkv_kernel,
        grid=(b, s // QKV_ROWS),
        in_specs=[
            pl.BlockSpec((1, QKV_ROWS, d), lambda i, t: (i, t, 0)),
            row, vec, vec,
            pl.BlockSpec((d, d), lambda i, t: (0, 0), **resident),
            pl.BlockSpec((2 * d, d), lambda i, t: (0, 0), **resident),
        ],
        out_specs=[
            transposed,
            pl.BlockSpec((1, QKV_ROWS, d), lambda i, t: (i, t, 0)),
            transposed,
            pl.BlockSpec((1, heads, nb, QKV_ROWS), lambda i, t: (i, 0, 0, t)),
        ],
        out_shape=[
            jax.ShapeDtypeStruct((b, d, s), BF16),
            jax.ShapeDtypeStruct((b, s, d), BF16),
            jax.ShapeDtypeStruct((b, d, s), BF16),
            jax.ShapeDtypeStruct((b, heads, nb, s), F32),
        ],
        scratch_shapes=[pltpu.VMEM((nb, d), F32)],
        compiler_params=_params(("parallel", "arbitrary")),
        name="qkv_select",
    )(x, gain.reshape(1, d), shift.reshape(b, 1, d), scale.reshape(b, 1, d), w_k, w_qvt)


def _bias_kernel(rb_ref, o_ref):
    head = pl.program_id(0)
    kk = lax.broadcasted_iota(jnp.int32, (MOBA_BLOCK, MOBA_BLOCK), 0)
    qq = lax.broadcasted_iota(jnp.int32, (MOBA_BLOCK, MOBA_BLOCK), 1)
    max_exact = NUM_BUCKETS // 2
    for off in range(NEAR_OFFSETS):
        dist = MOBA_BLOCK * off + qq - kk
        dpos = jnp.maximum(dist, 0)
        nf = jnp.maximum(dpos, 1).astype(F32)
        large = max_exact + (jnp.log(nf / max_exact) / math.log(MAX_DISTANCE / max_exact)
                             * (NUM_BUCKETS - max_exact)).astype(jnp.int32)
        large = jnp.minimum(large, NUM_BUCKETS - 1)
        bucket = jnp.where(dpos < max_exact, dpos, large)
        val = jnp.zeros((MOBA_BLOCK, MOBA_BLOCK), F32)
        for bkt in range(NUM_BUCKETS):
            val = jnp.where(bucket == bkt, rb_ref[bkt, head], val)
        val = val * LOG2E
        if off == 0:
            val = jnp.where(dist >= 0, val, -jnp.inf)
        o_ref[0, off] = val


def _bias_tiles(rel_bias):
    heads = rel_bias.shape[1]
    return pl.pallas_call(
        _bias_kernel,
        grid=(heads,),
        in_specs=[pl.BlockSpec(memory_space=pltpu.SMEM)],
        out_specs=pl.BlockSpec((1, NEAR_OFFSETS, MOBA_BLOCK, MOBA_BLOCK),
                               lambda h: (h, 0, 0, 0)),
        out_shape=jax.ShapeDtypeStruct((heads, NEAR_OFFSETS, MOBA_BLOCK, MOBA_BLOCK), F32),
        compiler_params=_params(("parallel",)),
        name="t5_bias_tiles",
    )(rel_bias)


def _attn_kernel(far_ref, qt_ref, k_ref, vt_ref, sel_ref, bias_ref, o_ref):
    hq = pl.program_id(1)
    i = pl.program_id(2)
    neg = -jnp.inf
    heads = range(HEADS_PER_STEP)
    ones_rows = jnp.ones((DENOM_ROWS, MOBA_BLOCK), BF16)
    dim_row = lax.broadcasted_iota(jnp.int32, (V7X_LANES, MOBA_BLOCK), 0)

    def slab_of(hh):
        slab = hh // HEADS_PER_SLAB
        return slice(slab * V7X_LANES, (slab + 1) * V7X_LANES)

    q_heads = []
    for hh in heads:
        lo = (hh % HEADS_PER_SLAB) * HEAD_DIM
        q_slab = qt_ref[0, slab_of(hh), :]
        own = (dim_row >= lo) & (dim_row < lo + HEAD_DIM)
        q_heads.append(jnp.where(own, q_slab, jnp.zeros_like(q_slab)))

    def raw_scores(hh, j):
        k_blk = k_ref[0, pl.ds(pl.multiple_of(j * MOBA_BLOCK, MOBA_BLOCK), MOBA_BLOCK), slab_of(hh)]
        return _dot(k_blk, q_heads[hh])

    def value_rows(hh, j):
        v_t = vt_ref[0, hh * HEAD_DIM:(hh + 1) * HEAD_DIM,
                     pl.ds(pl.multiple_of(j * MOBA_BLOCK, MOBA_BLOCK), MOBA_BLOCK)]
        return jnp.concatenate([v_t, ones_rows], axis=0)

    def select_row(hh, j):
        off = i - j
        row = jnp.where(off > 0, sel_ref[0, hh, pl.ds(j, 1), :], neg)
        return jnp.where(off == 0, 0.0, row)

    def far_bias(hh):
        return far_ref[hq * HEADS_PER_STEP + hh] * LOG2E

    def bias_tile(hh, j):
        off = i - j
        tile = bias_ref[hh, jnp.clip(off, 0, NEAR_OFFSETS - 1)]
        return jnp.where(off >= NEAR_OFFSETS, far_bias(hh), tile)

    def finish(accs):
        out_t = jnp.concatenate(
            [acc[:HEAD_DIM] * (1.0 / acc[HEAD_DIM:HEAD_DIM + 1]) for acc in accs], axis=0)
        return out_t.T.astype(BF16)

    def one_pass(accs, tiles, block_of, shift, ready=None):
        accs = list(accs)
        pending = []
        for t in range(len(tiles) + SCORE_LOOKAHEAD):
            if t < len(tiles):
                hh, n = tiles[t]
                if ready is not None and n == 0:
                    pending.append(ready[hh])
                else:
                    pending.append(raw_scores(hh, block_of(n)) + shift(hh, n))
            if t >= SCORE_LOOKAHEAD:
                hh, n = tiles[t - SCORE_LOOKAHEAD]
                p = jnp.exp2(pending[t - SCORE_LOOKAHEAD]).astype(BF16)
                accs[hh] = accs[hh] + _dot(value_rows(hh, block_of(n)), p)
        return tuple(accs)

    def near_pass(n_blocks):
        own = [raw_scores(hh, i) + bias_ref[hh, 0] for hh in heads]
        refs = [jnp.max(s, axis=0, keepdims=True) for s in own]

        def block_of(n):
            return jnp.maximum(i - n, 0)

        def shift(hh, n):
            row = jnp.where(i - n >= 0, sel_ref[0, hh, pl.ds(block_of(n), 1), :], neg)
            if n >= NEAR_OFFSETS:
                return row + (far_bias(hh) - refs[hh])
            return bias_ref[hh, n] + (row - refs[hh])

        zero = jnp.zeros((HEAD_DIM + DENOM_ROWS, MOBA_BLOCK), F32)
        tiles = [(hh, n) for n in range(n_blocks) for hh in heads]
        accs = one_pass((zero,) * HEADS_PER_STEP, tiles, block_of, shift,
                        [own[hh] - refs[hh] for hh in heads])
        return accs, tuple(refs)

    n_old = jnp.maximum(i - (NEAR_OFFSETS - 1), 0)
    accs, refs = lax.switch(
        n_old % KEY_GROUP,
        [functools.partial(near_pass, NEAR_OFFSETS + extra) for extra in range(KEY_GROUP)])

    def far_group(g, accs):
        def block_of(n):
            return g * KEY_GROUP + n

        def shift(hh, n):
            return sel_ref[0, hh, pl.ds(block_of(n), 1), :] + (far_bias(hh) - refs[hh])

        tiles = [(hh, n) for n in range(KEY_GROUP) for hh in heads]
        return one_pass(accs, tiles, block_of, shift)

    accs = lax.fori_loop(0, n_old // KEY_GROUP, far_group, accs)

    def running_max_pass():
        def visit(t, states):
            j = i - t
            out = []
            for hh in heads:
                m, acc = states[hh]
                s = raw_scores(hh, j) + bias_tile(hh, j) + select_row(hh, j)
                m_new = jnp.maximum(m, jnp.max(s, axis=0, keepdims=True))
                p = jnp.exp2(s - m_new).astype(BF16)
                out.append((m_new, jnp.exp2(m - m_new) * acc + _dot(value_rows(hh, j), p)))
            return tuple(out)

        init = ((jnp.full((1, MOBA_BLOCK), neg, F32),
                 jnp.zeros((HEAD_DIM + DENOM_ROWS, MOBA_BLOCK), F32)),) * HEADS_PER_STEP
        states = lax.fori_loop(0, i + 1, visit, init)
        return finish([acc for (_, acc) in states])

    largest = functools.reduce(jnp.maximum, [jnp.max(jnp.abs(acc)) for acc in accs])
    overflowed = jnp.logical_not(largest < jnp.inf)
    o_ref[0] = finish(accs)

    @pl.when(overflowed)
    def _():
        o_ref[0] = running_max_pass()


def _attention(q_t, k, v_t, select, bias_tiles, far_bias):
    b, d, s = q_t.shape
    nb = s // MOBA_BLOCK
    width = HEADS_PER_STEP * HEAD_DIM
    return pl.pallas_call(
        _attn_kernel,
        grid=(b, d // width, nb),
        in_specs=[
            pl.BlockSpec(memory_space=pltpu.SMEM),
            pl.BlockSpec((1, width, MOBA_BLOCK), lambda bi, hq, i: (bi, hq, i)),
            pl.BlockSpec((1, s, width), lambda bi, hq, i: (bi, 0, hq)),
            pl.BlockSpec((1, width, s), lambda bi, hq, i: (bi, hq, 0)),
            pl.BlockSpec((1, HEADS_PER_STEP, nb, MOBA_BLOCK), lambda bi, hq, i: (bi, hq, 0, i)),
            pl.BlockSpec((HEADS_PER_STEP, NEAR_OFFSETS, MOBA_BLOCK, MOBA_BLOCK),
                         lambda bi, hq, i: (hq, 0, 0, 0), pipeline_mode=pl.Buffered(1)),
        ],
        out_specs=pl.BlockSpec((1, MOBA_BLOCK, width), lambda bi, hq, i: (bi, i, hq)),
        out_shape=jax.ShapeDtypeStruct((b, s, d), BF16),
        compiler_params=_params(("parallel", "parallel", "parallel")),
        name="moba_attention",
    )(far_bias, q_t, k, v_t, select, bias_tiles)


def _oproj_kernel(x_ref, o_ref_in, w_ref, gate_ref, out_ref):
    out_ref[0] = x_ref[0] + gate_ref[0] * _dot(o_ref_in[0], w_ref[...])


def _oproj_layer(x, o, w_o, gate):
    b, s, d = x.shape
    tile = pl.BlockSpec((1, MLP_ROWS, d), lambda i, t: (i, t, 0))
    return pl.pallas_call(
        _oproj_kernel,
        grid=(b, s // MLP_ROWS),
        in_specs=[
            tile, tile,
            pl.BlockSpec((d, d), lambda i, t: (0, 0), pipeline_mode=pl.Buffered(1)),
            pl.BlockSpec((1, 1, d), lambda i, t: (i, 0, 0)),
        ],
        out_specs=tile,
        out_shape=jax.ShapeDtypeStruct((b, s, d), F32),
        compiler_params=_params(("parallel", "parallel")),
        name="attn_out_proj",
    )(x, o, w_o.astype(BF16), gate.reshape(b, 1, d))


def kernel(x, c, rel_bias, w_mod, b_mod, norm_mix, norm_mlp, w_pool, pool_scale,
           w_qkv, w_o, w_up, w_down, norm_final):
    b, s, d = x.shape
    assert d % (HEADS_PER_STEP * HEAD_DIM) == 0 and s % MLP_ROWS == 0
    assert s % QKV_ROWS == 0 and QKV_ROWS % MOBA_BLOCK == 0
    assert w_mod.shape[0] == 2 and w_mod.shape[2] % MOD_COLS == 0
    mod = _modulation(c, w_mod, b_mod)
    sh1, sc1, g1, sh2, sc2, g2 = (mod[:, :, n * d:(n + 1) * d] for n in range(6))

    x = _pool_layer(x, norm_mix[0], sh1[0], sc1[0], g1[0], w_pool[0], pool_scale[0])
    x = _mlp_layer(x, norm_mlp[0], sh2[0], sc2[0], g2[0], w_up[0], w_down[0], norm_final, False)

    q_t, k, v_t, select = _qkv_layer(x, norm_mix[1], sh1[1], sc1[1], w_qkv[0])
    o = _attention(q_t, k, v_t, select, _bias_tiles(rel_bias), rel_bias[NUM_BUCKETS - 1])
    x = _oproj_layer(x, o, w_o[0], g1[1])
    return _mlp_layer(x, norm_mlp[1], sh2[1], sc2[1], g2[1], w_up[1], w_down[1], norm_final, True)
```

```python
import functools
import math

import jax
import jax.numpy as jnp
from jax import lax
from jax.experimental import pallas as pl
from jax.experimental.pallas import tpu as pltpu

F32 = jnp.float32
BF16 = jnp.bfloat16

HEAD_DIM = 64
MOBA_BLOCK = 256
MOBA_TOPK = 3
NUM_BUCKETS = 32
MAX_DISTANCE = 1024
POOL_WINDOWS = (2, 4, 8, 16)
EPS = 1e-6

V7X_LANES = 128
V7X_SUBLANES = 8
V7X_VMEM_LIMIT_BYTES = 56 * 1024 * 1024

NEAR_OFFSETS = 5
KEY_GROUP = 4
DENOM_ROWS = 16
SCORE_LOOKAHEAD = 8
LOG2E = math.log2(math.e)
HEADS_PER_SLAB = V7X_LANES // HEAD_DIM
HEADS_PER_STEP = 4 * HEADS_PER_SLAB
POOL_HALO = max(POOL_WINDOWS)

MLP_ROWS = 512
MLP_FF_CHUNK = 512
QKV_ROWS = 512
MOD_COLS = 1536


def _params(semantics):
    return pltpu.CompilerParams(dimension_semantics=semantics,
                                vmem_limit_bytes=V7X_VMEM_LIMIT_BYTES)


def _dot(a, b):
    return jnp.dot(a, b, preferred_element_type=F32)


def _rms_modulate(x, gain, shift, scale):
    y = x * lax.rsqrt(jnp.mean(x * x, axis=-1, keepdims=True) + EPS)
    return (y * gain) * (1.0 + scale) + shift


def _mod_kernel(c_ref, w_ref, b_ref, o_ref):
    c = c_ref[...]
    a = c * (1.0 / (1.0 + jnp.exp(-c)))
    w = w_ref[0]
    a_hi = a.astype(BF16)
    a_lo = (a - a_hi.astype(F32)).astype(BF16)
    w_hi = w.astype(BF16)
    w_lo = (w - w_hi.astype(F32)).astype(BF16)
    acc = _dot(a_hi, w_hi) + _dot(a_hi, w_lo) + _dot(a_lo, w_hi)
    o_ref[0] = acc + b_ref[0]


def _modulation(c, w_mod, b_mod):
    depth, d, n = w_mod.shape
    b = c.shape[0]
    rows = -(-b // V7X_SUBLANES) * V7X_SUBLANES
    c_pad = jnp.pad(c, ((0, rows - b), (0, 0)))
    out = pl.pallas_call(
        _mod_kernel,
        grid=(depth, n // MOD_COLS),
        in_specs=[
            pl.BlockSpec((rows, d), lambda i, j: (0, 0)),
            pl.BlockSpec((1, d, MOD_COLS), lambda i, j: (i, 0, j)),
            pl.BlockSpec((1, 1, MOD_COLS), lambda i, j: (i, 0, j)),
        ],
        out_specs=pl.BlockSpec((1, rows, MOD_COLS), lambda i, j: (i, 0, j)),
        out_shape=jax.ShapeDtypeStruct((depth, rows, n), F32),
        compiler_params=_params(("parallel", "parallel")),
        name="modulation",
    )(c_pad, w_mod, b_mod.reshape(depth, 1, n))
    return out[:, :b]


def _pool_residual(x, halo, t, gain, shift, scale, gate, w_ref, pool_scale):
    rows = x.shape[0]
    group = w_ref.shape[1]
    h = _rms_modulate(x, gain, shift, scale)
    h_prev = jnp.where(t > 0, _rms_modulate(halo, gain, shift, scale), 0.0)
    h_ext = jnp.concatenate([h_prev, h], axis=0)
    pos = t * rows + lax.broadcasted_iota(jnp.int32, (rows, 1), 0)
    ys = []
    for g, window in enumerate(POOL_WINDOWS):
        cols = slice(g * group, (g + 1) * group)
        s = h_ext[:, cols]
        step = 1
        while step < window:
            s = s + pltpu.roll(s, step, axis=0)
            step *= 2
        inv_cnt = 1.0 / jnp.minimum(pos + 1, window).astype(F32)
        pooled = s[POOL_HALO:] * inv_cnt - h[:, cols]
        ys.append(_dot(pooled.astype(BF16), w_ref[g]))
    return x + gate * (jnp.concatenate(ys, axis=-1) * pool_scale)


def _mlp_residual(x, gain, shift, scale, gate, wup_ref, wdn_ref, h_scr, acc_scr):
    h_scr[...] = _rms_modulate(x, gain, shift, scale).astype(BF16)
    acc_scr[...] = jnp.zeros_like(acc_scr)

    def chunk(f, carry):
        f0 = pl.multiple_of(f * MLP_FF_CHUNK, MLP_FF_CHUNK)
        up = _dot(h_scr[...], wup_ref[:, pl.ds(f0, MLP_FF_CHUNK)])
        a = jnp.maximum(up, 0.0)
        acc_scr[...] += _dot((a * a).astype(BF16), wdn_ref[pl.ds(f0, MLP_FF_CHUNK), :])
        return carry

    lax.fori_loop(0, wup_ref.shape[1] // MLP_FF_CHUNK, chunk, 0, unroll=True)
    return x + gate * acc_scr[...]


def _pool_mlp_kernel(x_ref, halo_ref, gain1_ref, sh1_ref, sc1_ref, g1_ref, wpool_ref, ps_ref,
                     gain2_ref, sh2_ref, sc2_ref, g2_ref, wup_ref, wdn_ref, o_ref, h_scr, acc_scr):
    x = _pool_residual(x_ref[0], halo_ref[0], pl.program_id(1), gain1_ref[...], sh1_ref[0],
                       sc1_ref[0], g1_ref[0], wpool_ref, ps_ref[...])
    o_ref[0] = _mlp_residual(x, gain2_ref[...], sh2_ref[0], sc2_ref[0], g2_ref[0],
                             wup_ref, wdn_ref, h_scr, acc_scr)


def _attn_mlp_kernel(x_ref, attn_ref, wo_ref, g1_ref, gain2_ref, sh2_ref, sc2_ref, g2_ref,
                     wup_ref, wdn_ref, fin_ref, o_ref, h_scr, acc_scr):
    x = x_ref[0] + g1_ref[0] * _dot(attn_ref[0], wo_ref[...])
    out = _mlp_residual(x, gain2_ref[...], sh2_ref[0], sc2_ref[0], g2_ref[0],
                        wup_ref, wdn_ref, h_scr, acc_scr)
    out = out * lax.rsqrt(jnp.mean(out * out, axis=-1, keepdims=True) + EPS)
    o_ref[0] = out * fin_ref[...]


def _layer_specs(b, d):
    tile = pl.BlockSpec((1, MLP_ROWS, d), lambda i, t: (i, t, 0))
    vec = pl.BlockSpec((1, 1, d), lambda i, t: (i, 0, 0))
    row = pl.BlockSpec((1, d), lambda i, t: (0, 0))
    scratch = [pltpu.VMEM((MLP_ROWS, d), BF16), pltpu.VMEM((MLP_ROWS, d), F32)]
    return tile, vec, row, scratch


def _resident(shape):
    return pl.BlockSpec(shape, lambda i, t: (0,) * len(shape), pipeline_mode=pl.Buffered(1))


def _pool_mlp_layer(x, gain1, sh1, sc1, g1, w_pool, pool_scale, gain2, sh2, sc2, g2, w_up, w_down):
    b, s, d = x.shape
    tile, vec, row, scratch = _layer_specs(b, d)
    halo_blocks = MLP_ROWS // POOL_HALO
    halo = pl.BlockSpec((1, POOL_HALO, d), lambda i, t: (i, jnp.maximum(t * halo_blocks - 1, 0), 0))
    per_batch = lambda v: v.reshape(b, 1, d)
    return pl.pallas_call(
        _pool_mlp_kernel,
        grid=(b, s // MLP_ROWS),
        in_specs=[tile, halo, row, vec, vec, vec, _resident(w_pool.shape), row,
                  row, vec, vec, vec, _resident(w_up.shape), _resident(w_down.shape)],
        out_specs=tile,
        out_shape=jax.ShapeDtypeStruct((b, s, d), F32),
        scratch_shapes=scratch,
        compiler_params=_params(("parallel", "parallel")),
        name="pool_mlp",
    )(x, x, gain1.reshape(1, d), per_batch(sh1), per_batch(sc1), per_batch(g1),
      w_pool.astype(BF16), pool_scale.reshape(1, d),
      gain2.reshape(1, d), per_batch(sh2), per_batch(sc2), per_batch(g2),
      w_up.astype(BF16), w_down.astype(BF16))


def _attn_mlp_layer(x, attn, w_o, g1, gain2, sh2, sc2, g2, w_up, w_down, norm_final):
    b, s, d = x.shape
    tile, vec, row, scratch = _layer_specs(b, d)
    per_batch = lambda v: v.reshape(b, 1, d)
    return pl.pallas_call(
        _attn_mlp_kernel,
        grid=(b, s // MLP_ROWS),
        in_specs=[tile, tile, _resident(w_o.shape), vec, row, vec, vec, vec,
                  _resident(w_up.shape), _resident(w_down.shape), row],
        out_specs=tile,
        out_shape=jax.ShapeDtypeStruct((b, s, d), F32),
        scratch_shapes=scratch,
        compiler_params=_params(("parallel", "parallel")),
        name="attn_proj_mlp_norm",
    )(x, attn, w_o.astype(BF16), per_batch(g1), gain2.reshape(1, d), per_batch(sh2),
      per_batch(sc2), per_batch(g2), w_up.astype(BF16), w_down.astype(BF16),
      norm_final.reshape(1, d))


def _qkv_kernel(x_ref, gain_ref, sh_ref, sc_ref, wk_ref, wqvt_ref,
                qt_ref, k_ref, vt_ref, sel_ref, km_ref):
    t = pl.program_id(1)
    d = x_ref.shape[2]
    rows = x_ref.shape[1]
    blocks = rows // MOBA_BLOCK
    n_blocks = km_ref.shape[0]
    neg = -jnp.inf
    h = _rms_modulate(x_ref[0], gain_ref[...], sh_ref[0], sc_ref[0]).astype(BF16)
    k = _dot(h, wk_ref[...])
    k_ref[0] = k.astype(BF16)
    k_mean = jnp.mean(k.reshape(blocks, MOBA_BLOCK, d), axis=1)

    @pl.when(t == 0)
    def _():
        km_ref[...] = jnp.zeros_like(km_ref)

    for step in range(n_blocks // blocks):
        @pl.when(t == step)
        def _():
            km_ref[step * blocks:(step + 1) * blocks, :] = k_mean

    lane = lax.broadcasted_iota(jnp.int32, (n_blocks, V7X_LANES), 1)
    blk = lax.broadcasted_iota(jnp.int32, (n_blocks, rows), 0)
    own_blk = (t * rows + lax.broadcasted_iota(jnp.int32, (n_blocks, rows), 1)) // MOBA_BLOCK
    blk_f = blk.astype(F32)

    def transposed(w_rows):
        return lax.dot_general(wqvt_ref[w_rows, :], h, (((1,), (1,)), ((), ())),
                               preferred_element_type=F32)

    q_t = (transposed(slice(0, d)) * (HEAD_DIM ** -0.5 * LOG2E)).astype(BF16)
    qt_ref[0] = q_t
    gates = []
    for head in range(d // HEAD_DIM):
        slab = slice(head // HEADS_PER_SLAB * V7X_LANES, (head // HEADS_PER_SLAB + 1) * V7X_LANES)
        lo = (head % HEADS_PER_SLAB) * HEAD_DIM
        in_head = (lane >= lo) & (lane < lo + HEAD_DIM)
        means = jnp.where(in_head, km_ref[:, slab], 0.0).astype(BF16)
        gates.append(jnp.where(blk < own_blk, _dot(means, q_t[slab]), neg))
    vt_ref[0] = transposed(slice(d, 2 * d)).astype(BF16)
    for head, gate in enumerate(gates):
        chosen = jnp.full((n_blocks, rows), neg, F32)
        for _ in range(MOBA_TOPK):
            best = jnp.max(gate, axis=0, keepdims=True)
            first = jnp.min(jnp.where(gate == best, blk_f, float(n_blocks)), axis=0, keepdims=True)
            hit = blk_f == first
            chosen = jnp.where(hit & (best > neg), 0.0, chosen)
            gate = jnp.where(hit, neg, gate)
        sel_ref[0, head] = chosen


def _qkv_layer(x, gain, shift, scale, w_qkv):
    b, s, d = x.shape
    nb = s // MOBA_BLOCK
    heads = d // HEAD_DIM
    vec = pl.BlockSpec((1, 1, d), lambda i, t: (i, 0, 0))
    row = pl.BlockSpec((1, d), lambda i, t: (0, 0))
    resident = dict(pipeline_mode=pl.Buffered(1))
    w_k = w_qkv[:, d:2 * d].astype(BF16)
    w_qvt = jnp.concatenate([w_qkv[:, :d], w_qkv[:, 2 * d:]], axis=1).T.astype(BF16)
    transposed = pl.BlockSpec((1, d, QKV_ROWS), lambda i, t: (i, 0, t))
    return pl.pallas_call(
        _qkv_kernel,
        grid=(b, s // QKV_ROWS),
        in_specs=[
            pl.BlockSpec((1, QKV_ROWS, d), lambda i, t: (i, t, 0)),
            row, vec, vec,
            pl.BlockSpec((d, d), lambda i, t: (0, 0), **resident),
            pl.BlockSpec((2 * d, d), lambda i, t: (0, 0), **resident),
        ],
        out_specs=[
            transposed,
            pl.BlockSpec((1, QKV_ROWS, d), lambda i, t: (i, t, 0)),
            transposed,
            pl.BlockSpec((1, heads, nb, QKV_ROWS), lambda i, t: (i, 0, 0, t)),
        ],
        out_shape=[
            jax.ShapeDtypeStruct((b, d, s), BF16),
            jax.ShapeDtypeStruct((b, s, d), BF16),
            jax.ShapeDtypeStruct((b, d, s), BF16),
            jax.ShapeDtypeStruct((b, heads, nb, s), F32),
        ],
        scratch_shapes=[pltpu.VMEM((nb, d), F32)],
        compiler_params=_params(("parallel", "arbitrary")),
        name="qkv_select",
    )(x, gain.reshape(1, d), shift.reshape(b, 1, d), scale.reshape(b, 1, d), w_k, w_qvt)


def _bias_kernel(rb_ref, o_ref):
    head = pl.program_id(0)
    span = 2 * MOBA_BLOCK
    max_exact = NUM_BUCKETS // 2
    for off in range(NEAR_OFFSETS):
        dist = MOBA_BLOCK * off - (MOBA_BLOCK - 1) + lax.broadcasted_iota(jnp.int32, (1, span), 1)
        dpos = jnp.maximum(dist, 0)
        nf = jnp.maximum(dpos, 1).astype(F32)
        large = max_exact + (jnp.log(nf / max_exact) / math.log(MAX_DISTANCE / max_exact)
                             * (NUM_BUCKETS - max_exact)).astype(jnp.int32)
        large = jnp.minimum(large, NUM_BUCKETS - 1)
        bucket = jnp.where(dpos < max_exact, dpos, large)
        val = jnp.zeros((1, span), F32)
        for bkt in range(NUM_BUCKETS):
            val = jnp.where(bucket == bkt, rb_ref[bkt, head], val)
        val = val * LOG2E
        if off == 0:
            val = jnp.where(dist >= 0, val, -jnp.inf)
        rolled = pltpu.roll(jnp.broadcast_to(val, (MOBA_BLOCK, span)), MOBA_BLOCK + 1, 1,
                            stride=1, stride_axis=0)
        o_ref[0, off] = rolled[:, :MOBA_BLOCK]


def _bias_tiles(rel_bias):
    heads = rel_bias.shape[1]
    return pl.pallas_call(
        _bias_kernel,
        grid=(heads,),
        in_specs=[pl.BlockSpec(memory_space=pltpu.SMEM)],
        out_specs=pl.BlockSpec((1, NEAR_OFFSETS, MOBA_BLOCK, MOBA_BLOCK),
                               lambda h: (h, 0, 0, 0)),
        out_shape=jax.ShapeDtypeStruct((heads, NEAR_OFFSETS, MOBA_BLOCK, MOBA_BLOCK), F32),
        compiler_params=_params(("parallel",)),
        name="t5_bias_tiles",
    )(rel_bias)


def _attn_kernel(far_ref, qt_ref, k_ref, vt_ref, sel_ref, bias_ref, o_ref):
    hq = pl.program_id(1)
    i = pl.program_id(2)
    neg = -jnp.inf
    heads = range(HEADS_PER_STEP)
    ones_rows = jnp.ones((DENOM_ROWS, MOBA_BLOCK), BF16)
    dim_row = lax.broadcasted_iota(jnp.int32, (V7X_LANES, MOBA_BLOCK), 0)

    def slab_of(hh):
        slab = hh // HEADS_PER_SLAB
        return slice(slab * V7X_LANES, (slab + 1) * V7X_LANES)

    q_heads = []
    for hh in heads:
        lo = (hh % HEADS_PER_SLAB) * HEAD_DIM
        q_slab = qt_ref[0, slab_of(hh), :]
        own = (dim_row >= lo) & (dim_row < lo + HEAD_DIM)
        q_heads.append(jnp.where(own, q_slab, jnp.zeros_like(q_slab)))

    def raw_scores(hh, j):
        k_blk = k_ref[0, pl.ds(pl.multiple_of(j * MOBA_BLOCK, MOBA_BLOCK), MOBA_BLOCK), slab_of(hh)]
        return _dot(k_blk, q_heads[hh])

    def value_rows(hh, j):
        v_t = vt_ref[0, hh * HEAD_DIM:(hh + 1) * HEAD_DIM,
                     pl.ds(pl.multiple_of(j * MOBA_BLOCK, MOBA_BLOCK), MOBA_BLOCK)]
        return jnp.concatenate([v_t, ones_rows], axis=0)

    def select_row(hh, j):
        off = i - j
        row = jnp.where(off > 0, sel_ref[0, hh, pl.ds(j, 1), :], neg)
        return jnp.where(off == 0, 0.0, row)

    def far_bias(hh):
        return far_ref[hq * HEADS_PER_STEP + hh] * LOG2E

    def bias_tile(hh, j):
        off = i - j
        tile = bias_ref[hh, jnp.clip(off, 0, NEAR_OFFSETS - 1)]
        return jnp.where(off >= NEAR_OFFSETS, far_bias(hh), tile)

    def finish(accs):
        out_t = jnp.concatenate(
            [acc[:HEAD_DIM] * (1.0 / acc[HEAD_DIM:HEAD_DIM + 1]) for acc in accs], axis=0)
        return out_t.T.astype(BF16)

    def one_pass(accs, tiles, block_of, shift, ready=None):
        accs = list(accs)
        pending = []
        for t in range(len(tiles) + SCORE_LOOKAHEAD):
            if t < len(tiles):
                hh, n = tiles[t]
                if ready is not None and n == 0:
                    pending.append(ready[hh])
                else:
                    pending.append(raw_scores(hh, block_of(n)) + shift(hh, n))
            if t >= SCORE_LOOKAHEAD:
                hh, n = tiles[t - SCORE_LOOKAHEAD]
                p = jnp.exp2(pending[t - SCORE_LOOKAHEAD]).astype(BF16)
                accs[hh] = accs[hh] + _dot(value_rows(hh, block_of(n)), p)
        return tuple(accs)

    def near_pass(n_blocks):
        own = [raw_scores(hh, i) + bias_ref[hh, 0] for hh in heads]
        refs = [jnp.max(s, axis=0, keepdims=True) for s in own]

        def block_of(n):
            return jnp.maximum(i - n, 0)

        def shift(hh, n):
            row = jnp.where(i - n >= 0, sel_ref[0, hh, pl.ds(block_of(n), 1), :], neg)
            if n >= NEAR_OFFSETS:
                return row + (far_bias(hh) - refs[hh])
            return bias_ref[hh, n] + (row - refs[hh])

        zero = jnp.zeros((HEAD_DIM + DENOM_ROWS, MOBA_BLOCK), F32)
        tiles = [(hh, n) for n in range(n_blocks) for hh in heads]
        accs = one_pass((zero,) * HEADS_PER_STEP, tiles, block_of, shift,
                        [own[hh] - refs[hh] for hh in heads])
        return accs, tuple(refs)

    n_old = jnp.maximum(i - (NEAR_OFFSETS - 1), 0)
    accs, refs = lax.switch(
        n_old % KEY_GROUP,
        [functools.partial(near_pass, NEAR_OFFSETS + extra) for extra in range(KEY_GROUP)])

    def far_group(g, accs):
        def block_of(n):
            return g * KEY_GROUP + n

        def shift(hh, n):
            return sel_ref[0, hh, pl.ds(block_of(n), 1), :] + (far_bias(hh) - refs[hh])

        tiles = [(hh, n) for n in range(KEY_GROUP) for hh in heads]
        return one_pass(accs, tiles, block_of, shift)

    accs = lax.fori_loop(0, n_old // KEY_GROUP, far_group, accs)

    def running_max_pass():
        def visit(t, states):
            j = i - t
            out = []
            for hh in heads:
                m, acc = states[hh]
                s = raw_scores(hh, j) + bias_tile(hh, j) + select_row(hh, j)
                m_new = jnp.maximum(m, jnp.max(s, axis=0, keepdims=True))
                p = jnp.exp2(s - m_new).astype(BF16)
                out.append((m_new, jnp.exp2(m - m_new) * acc + _dot(value_rows(hh, j), p)))
            return tuple(out)

        init = ((jnp.full((1, MOBA_BLOCK), neg, F32),
                 jnp.zeros((HEAD_DIM + DENOM_ROWS, MOBA_BLOCK), F32)),) * HEADS_PER_STEP
        states = lax.fori_loop(0, i + 1, visit, init)
        return finish([acc for (_, acc) in states])

    largest = functools.reduce(jnp.maximum, [jnp.max(jnp.abs(acc)) for acc in accs])
    overflowed = jnp.logical_not(largest < jnp.inf)
    o_ref[0] = finish(accs)

    @pl.when(overflowed)
    def _():
        o_ref[0] = running_max_pass()


def _attention(q_t, k, v_t, select, bias_tiles, far_bias):
    b, d, s = q_t.shape
    nb = s // MOBA_BLOCK
    width = HEADS_PER_STEP * HEAD_DIM
    return pl.pallas_call(
        _attn_kernel,
        grid=(b, d // width, nb),
        in_specs=[
            pl.BlockSpec(memory_space=pltpu.SMEM),
            pl.BlockSpec((1, width, MOBA_BLOCK), lambda bi, hq, i: (bi, hq, i)),
            pl.BlockSpec((1, s, width), lambda bi, hq, i: (bi, 0, hq)),
            pl.BlockSpec((1, width, s), lambda bi, hq, i: (bi, hq, 0)),
            pl.BlockSpec((1, HEADS_PER_STEP, nb, MOBA_BLOCK), lambda bi, hq, i: (bi, hq, 0, i)),
            pl.BlockSpec((HEADS_PER_STEP, NEAR_OFFSETS, MOBA_BLOCK, MOBA_BLOCK),
                         lambda bi, hq, i: (hq, 0, 0, 0), pipeline_mode=pl.Buffered(1)),
        ],
        out_specs=pl.BlockSpec((1, MOBA_BLOCK, width), lambda bi, hq, i: (bi, i, hq)),
        out_shape=jax.ShapeDtypeStruct((b, s, d), BF16),
        compiler_params=_params(("parallel", "parallel", "parallel")),
        name="moba_attention",
    )(far_bias, q_t, k, v_t, select, bias_tiles)


def kernel(x, c, rel_bias, w_mod, b_mod, norm_mix, norm_mlp, w_pool, pool_scale,
           w_qkv, w_o, w_up, w_down, norm_final):
    b, s, d = x.shape
    assert d % (HEADS_PER_STEP * HEAD_DIM) == 0 and s % MLP_ROWS == 0
    assert s % QKV_ROWS == 0 and QKV_ROWS % MOBA_BLOCK == 0
    assert w_mod.shape[0] == 2 and w_mod.shape[2] % MOD_COLS == 0
    mod = _modulation(c, w_mod, b_mod)
    sh1, sc1, g1, sh2, sc2, g2 = (mod[:, :, n * d:(n + 1) * d] for n in range(6))

    x = _pool_mlp_layer(x, norm_mix[0], sh1[0], sc1[0], g1[0], w_pool[0], pool_scale[0],
                        norm_mlp[0], sh2[0], sc2[0], g2[0], w_up[0], w_down[0])

    q_t, k, v_t, select = _qkv_layer(x, norm_mix[1], sh1[1], sc1[1], w_qkv[0])
    attn = _attention(q_t, k, v_t, select, _bias_tiles(rel_bias), rel_bias[NUM_BUCKETS - 1])
    return _attn_mlp_layer(x, attn, w_o[0], g1[1], norm_mlp[1], sh2[1], sc2[1], g2[1],
                           w_up[1], w_down[1], norm_final)
```

```python
import functools
import math

import jax
import jax.numpy as jnp
from jax import lax
from jax.experimental import pallas as pl
from jax.experimental.pallas import tpu as pltpu

F32 = jnp.float32
BF16 = jnp.bfloat16

HEAD_DIM = 64
MOBA_BLOCK = 256
MOBA_TOPK = 3
NUM_BUCKETS = 32
MAX_DISTANCE = 1024
POOL_WINDOWS = (2, 4, 8, 16)
EPS = 1e-6

V7X_LANES = 128
V7X_SUBLANES = 8
V7X_VMEM_LIMIT_BYTES = 56 * 1024 * 1024

NEAR_OFFSETS = 5
KEY_GROUP = 4
DENOM_ROWS = 16
SCORE_LOOKAHEAD = 8
LOG2E = math.log2(math.e)
HEADS_PER_SLAB = V7X_LANES // HEAD_DIM
HEADS_PER_STEP = 4 * HEADS_PER_SLAB
POOL_HALO = max(POOL_WINDOWS)

MLP_ROWS = 512
MLP_FF_CHUNK = 512
QKV_ROWS = 512
MOD_COLS = 1536


def _params(semantics):
    return pltpu.CompilerParams(dimension_semantics=semantics,
                                vmem_limit_bytes=V7X_VMEM_LIMIT_BYTES)


def _dot(a, b):
    return jnp.dot(a, b, preferred_element_type=F32)


def _rms_modulate(x, gain, shift, scale):
    y = x * lax.rsqrt(jnp.mean(x * x, axis=-1, keepdims=True) + EPS)
    return (y * gain) * (1.0 + scale) + shift


def _mod_kernel(c_ref, w_ref, b_ref, o_ref):
    c = c_ref[...]
    a = c * (1.0 / (1.0 + jnp.exp(-c)))
    w = w_ref[0]
    a_hi = a.astype(BF16)
    a_lo = (a - a_hi.astype(F32)).astype(BF16)
    w_hi = w.astype(BF16)
    w_lo = (w - w_hi.astype(F32)).astype(BF16)
    acc = _dot(a_hi, w_hi) + _dot(a_hi, w_lo) + _dot(a_lo, w_hi)
    o_ref[0] = acc + b_ref[0]


def _modulation(c, w_mod, b_mod):
    depth, d, n = w_mod.shape
    b = c.shape[0]
    rows = -(-b // V7X_SUBLANES) * V7X_SUBLANES
    c_pad = jnp.pad(c, ((0, rows - b), (0, 0)))
    out = pl.pallas_call(
        _mod_kernel,
        grid=(depth, n // MOD_COLS),
        in_specs=[
            pl.BlockSpec((rows, d), lambda i, j: (0, 0)),
            pl.BlockSpec((1, d, MOD_COLS), lambda i, j: (i, 0, j)),
            pl.BlockSpec((1, 1, MOD_COLS), lambda i, j: (i, 0, j)),
        ],
        out_specs=pl.BlockSpec((1, rows, MOD_COLS), lambda i, j: (i, 0, j)),
        out_shape=jax.ShapeDtypeStruct((depth, rows, n), F32),
        compiler_params=_params(("parallel", "parallel")),
        name="modulation",
    )(c_pad, w_mod, b_mod.reshape(depth, 1, n))
    return out[:, :b]


def _pool_residual(x, halo, t, gain, shift, scale, gate, w_ref, pool_scale):
    rows = x.shape[0]
    group = w_ref.shape[1]
    h = _rms_modulate(x, gain, shift, scale)
    h_prev = jnp.where(t > 0, _rms_modulate(halo, gain, shift, scale), 0.0)
    h_ext = jnp.concatenate([h_prev, h], axis=0)
    pos = t * rows + lax.broadcasted_iota(jnp.int32, (rows, 1), 0)
    ys = []
    for g, window in enumerate(POOL_WINDOWS):
        cols = slice(g * group, (g + 1) * group)
        s = h_ext[:, cols]
        step = 1
        while step < window:
            s = s + pltpu.roll(s, step, axis=0)
            step *= 2
        inv_cnt = 1.0 / jnp.minimum(pos + 1, window).astype(F32)
        pooled = s[POOL_HALO:] * inv_cnt - h[:, cols]
        ys.append(_dot(pooled.astype(BF16), w_ref[g]))
    return x + gate * (jnp.concatenate(ys, axis=-1) * pool_scale)


def _mlp_residual(x, gain, shift, scale, gate, wup_ref, wdn_ref, h_scr, acc_scr):
    h_scr[...] = _rms_modulate(x, gain, shift, scale).astype(BF16)
    acc_scr[...] = jnp.zeros_like(acc_scr)

    def chunk(f, carry):
        f0 = pl.multiple_of(f * MLP_FF_CHUNK, MLP_FF_CHUNK)
        up = _dot(h_scr[...], wup_ref[:, pl.ds(f0, MLP_FF_CHUNK)])
        a = jnp.maximum(up, 0.0)
        acc_scr[...] += _dot((a * a).astype(BF16), wdn_ref[pl.ds(f0, MLP_FF_CHUNK), :])
        return carry

    lax.fori_loop(0, wup_ref.shape[1] // MLP_FF_CHUNK, chunk, 0, unroll=True)
    return x + gate * acc_scr[...]


def _pool_mlp_kernel(x_ref, halo_ref, gain1_ref, sh1_ref, sc1_ref, g1_ref, wpool_ref, ps_ref,
                     gain2_ref, sh2_ref, sc2_ref, g2_ref, wup_ref, wdn_ref, o_ref, h_scr, acc_scr):
    x = _pool_residual(x_ref[0], halo_ref[0], pl.program_id(1), gain1_ref[...], sh1_ref[0],
                       sc1_ref[0], g1_ref[0], wpool_ref, ps_ref[...])
    o_ref[0] = _mlp_residual(x, gain2_ref[...], sh2_ref[0], sc2_ref[0], g2_ref[0],
                             wup_ref, wdn_ref, h_scr, acc_scr)


def _attn_mlp_kernel(x_ref, attn_ref, wo_ref, g1_ref, gain2_ref, sh2_ref, sc2_ref, g2_ref,
                     wup_ref, wdn_ref, fin_ref, o_ref, h_scr, acc_scr):
    x = x_ref[0] + g1_ref[0] * _dot(attn_ref[0], wo_ref[...])
    out = _mlp_residual(x, gain2_ref[...], sh2_ref[0], sc2_ref[0], g2_ref[0],
                        wup_ref, wdn_ref, h_scr, acc_scr)
    out = out * lax.rsqrt(jnp.mean(out * out, axis=-1, keepdims=True) + EPS)
    o_ref[0] = out * fin_ref[...]


def _layer_specs(b, d):
    tile = pl.BlockSpec((1, MLP_ROWS, d), lambda i, t: (i, t, 0))
    vec = pl.BlockSpec((1, 1, d), lambda i, t: (i, 0, 0))
    row = pl.BlockSpec((1, d), lambda i, t: (0, 0))
    scratch = [pltpu.VMEM((MLP_ROWS, d), BF16), pltpu.VMEM((MLP_ROWS, d), F32)]
    return tile, vec, row, scratch


def _resident(shape):
    return pl.BlockSpec(shape, lambda i, t: (0,) * len(shape), pipeline_mode=pl.Buffered(1))


def _pool_mlp_layer(x, gain1, sh1, sc1, g1, w_pool, pool_scale, gain2, sh2, sc2, g2, w_up, w_down):
    b, s, d = x.shape
    tile, vec, row, scratch = _layer_specs(b, d)
    halo_blocks = MLP_ROWS // POOL_HALO
    halo = pl.BlockSpec((1, POOL_HALO, d), lambda i, t: (i, jnp.maximum(t * halo_blocks - 1, 0), 0))
    per_batch = lambda v: v.reshape(b, 1, d)
    return pl.pallas_call(
        _pool_mlp_kernel,
        grid=(b, s // MLP_ROWS),
        in_specs=[tile, halo, row, vec, vec, vec, _resident(w_pool.shape), row,
                  row, vec, vec, vec, _resident(w_up.shape), _resident(w_down.shape)],
        out_specs=tile,
        out_shape=jax.ShapeDtypeStruct((b, s, d), F32),
        scratch_shapes=scratch,
        compiler_params=_params(("parallel", "parallel")),
        name="pool_mlp",
    )(x, x, gain1.reshape(1, d), per_batch(sh1), per_batch(sc1), per_batch(g1),
      w_pool.astype(BF16), pool_scale.reshape(1, d),
      gain2.reshape(1, d), per_batch(sh2), per_batch(sc2), per_batch(g2),
      w_up.astype(BF16), w_down.astype(BF16))


def _attn_mlp_layer(x, attn, w_o, g1, gain2, sh2, sc2, g2, w_up, w_down, norm_final):
    b, s, d = x.shape
    tile, vec, row, scratch = _layer_specs(b, d)
    per_batch = lambda v: v.reshape(b, 1, d)
    return pl.pallas_call(
        _attn_mlp_kernel,
        grid=(b, s // MLP_ROWS),
        in_specs=[tile, tile, _resident(w_o.shape), vec, row, vec, vec, vec,
                  _resident(w_up.shape), _resident(w_down.shape), row],
        out_specs=tile,
        out_shape=jax.ShapeDtypeStruct((b, s, d), F32),
        scratch_shapes=scratch,
        compiler_params=_params(("parallel", "parallel")),
        name="attn_proj_mlp_norm",
    )(x, attn, w_o.astype(BF16), per_batch(g1), gain2.reshape(1, d), per_batch(sh2),
      per_batch(sc2), per_batch(g2), w_up.astype(BF16), w_down.astype(BF16),
      norm_final.reshape(1, d))


def _qkv_kernel(x_ref, gain_ref, sh_ref, sc_ref, wk_ref, wqvt_ref,
                qt_ref, k_ref, vt_ref, sel_ref, km_ref):
    t = pl.program_id(1)
    d = x_ref.shape[2]
    rows = x_ref.shape[1]
    blocks = rows // MOBA_BLOCK
    n_blocks = km_ref.shape[0]
    neg = -jnp.inf
    h = _rms_modulate(x_ref[0], gain_ref[...], sh_ref[0], sc_ref[0]).astype(BF16)
    k = _dot(h, wk_ref[...])
    k_ref[0] = k.astype(BF16)
    k_mean = jnp.mean(k.reshape(blocks, MOBA_BLOCK, d), axis=1)

    @pl.when(t == 0)
    def _():
        km_ref[...] = jnp.zeros_like(km_ref)

    for step in range(n_blocks // blocks):
        @pl.when(t == step)
        def _():
            km_ref[step * blocks:(step + 1) * blocks, :] = k_mean

    lane = lax.broadcasted_iota(jnp.int32, (n_blocks, V7X_LANES), 1)
    blk = lax.broadcasted_iota(jnp.int32, (n_blocks, rows), 0)
    own_blk = (t * rows + lax.broadcasted_iota(jnp.int32, (n_blocks, rows), 1)) // MOBA_BLOCK
    blk_f = blk.astype(F32)

    def transposed(w_rows):
        return lax.dot_general(wqvt_ref[w_rows, :], h, (((1,), (1,)), ((), ())),
                               preferred_element_type=F32)

    q_t = (transposed(slice(0, d)) * (HEAD_DIM ** -0.5 * LOG2E)).astype(BF16)
    qt_ref[0] = q_t
    gates = []
    for head in range(d // HEAD_DIM):
        slab = slice(head // HEADS_PER_SLAB * V7X_LANES, (head // HEADS_PER_SLAB + 1) * V7X_LANES)
        lo = (head % HEADS_PER_SLAB) * HEAD_DIM
        in_head = (lane >= lo) & (lane < lo + HEAD_DIM)
        means = jnp.where(in_head, km_ref[:, slab], 0.0).astype(BF16)
        gates.append(jnp.where(blk < own_blk, _dot(means, q_t[slab]), neg))
    vt_ref[0] = transposed(slice(d, 2 * d)).astype(BF16)
    for head, gate in enumerate(gates):
        chosen = jnp.full((n_blocks, rows), neg, F32)
        for _ in range(MOBA_TOPK):
            best = jnp.max(gate, axis=0, keepdims=True)
            first = jnp.min(jnp.where(gate == best, blk_f, float(n_blocks)), axis=0, keepdims=True)
            hit = blk_f == first
            chosen = jnp.where(hit & (best > neg), 0.0, chosen)
            gate = jnp.where(hit, neg, gate)
        sel_ref[0, head] = chosen


def _qkv_layer(x, gain, shift, scale, w_qkv):
    b, s, d = x.shape
    nb = s // MOBA_BLOCK
    heads = d // HEAD_DIM
    vec = pl.BlockSpec((1, 1, d), lambda i, t: (i, 0, 0))
    row = pl.BlockSpec((1, d), lambda i, t: (0, 0))
    resident = dict(pipeline_mode=pl.Buffered(1))
    w_k = w_qkv[:, d:2 * d].astype(BF16)
    w_qvt = jnp.concatenate([w_qkv[:, :d], w_qkv[:, 2 * d:]], axis=1).T.astype(BF16)
    transposed = pl.BlockSpec((1, d, QKV_ROWS), lambda i, t: (i, 0, t))
    return pl.pallas_call(
        _qkv_kernel,
        grid=(b, s // QKV_ROWS),
        in_specs=[
            pl.BlockSpec((1, QKV_ROWS, d), lambda i, t: (i, t, 0)),
            row, vec, vec,
            pl.BlockSpec((d, d), lambda i, t: (0, 0), **resident),
            pl.BlockSpec((2 * d, d), lambda i, t: (0, 0), **resident),
        ],
        out_specs=[
            transposed,
            pl.BlockSpec((1, QKV_ROWS, d), lambda i, t: (i, t, 0)),
            transposed,
            pl.BlockSpec((1, heads, nb, QKV_ROWS), lambda i, t: (i, 0, 0, t)),
        ],
        out_shape=[
            jax.ShapeDtypeStruct((b, d, s), BF16),
            jax.ShapeDtypeStruct((b, s, d), BF16),
            jax.ShapeDtypeStruct((b, d, s), BF16),
            jax.ShapeDtypeStruct((b, heads, nb, s), F32),
        ],
        scratch_shapes=[pltpu.VMEM((nb, d), F32)],
        compiler_params=_params(("parallel", "arbitrary")),
        name="qkv_select",
    )(x, gain.reshape(1, d), shift.reshape(b, 1, d), scale.reshape(b, 1, d), w_k, w_qvt)


def _bias_kernel(rb_ref, o_ref):
    head = pl.program_id(0)
    span = 2 * MOBA_BLOCK
    max_exact = NUM_BUCKETS // 2
    for off in range(NEAR_OFFSETS):
        dist = MOBA_BLOCK * off - (MOBA_BLOCK - 1) + lax.broadcasted_iota(jnp.int32, (1, span), 1)
        dpos = jnp.maximum(dist, 0)
        nf = jnp.maximum(dpos, 1).astype(F32)
        large = max_exact + (jnp.log(nf / max_exact) / math.log(MAX_DISTANCE / max_exact)
                             * (NUM_BUCKETS - max_exact)).astype(jnp.int32)
        large = jnp.minimum(large, NUM_BUCKETS - 1)
        bucket = jnp.where(dpos < max_exact, dpos, large)
        val = jnp.zeros((1, span), F32)
        for bkt in range(NUM_BUCKETS):
            val = jnp.where(bucket == bkt, rb_ref[bkt, head], val)
        val = val * LOG2E
        if off == 0:
            val = jnp.where(dist >= 0, val, -jnp.inf)
        rolled = pltpu.roll(jnp.broadcast_to(val, (MOBA_BLOCK, span)), MOBA_BLOCK + 1, 1,
                            stride=1, stride_axis=0)
        o_ref[0, off] = rolled[:, :MOBA_BLOCK]


def _bias_tiles(rel_bias):
    heads = rel_bias.shape[1]
    return pl.pallas_call(
        _bias_kernel,
        grid=(heads,),
        in_specs=[pl.BlockSpec(memory_space=pltpu.SMEM)],
        out_specs=pl.BlockSpec((1, NEAR_OFFSETS, MOBA_BLOCK, MOBA_BLOCK),
                               lambda h: (h, 0, 0, 0)),
        out_shape=jax.ShapeDtypeStruct((heads, NEAR_OFFSETS, MOBA_BLOCK, MOBA_BLOCK), F32),
        compiler_params=_params(("parallel",)),
        name="t5_bias_tiles",
    )(rel_bias)


def _attn_kernel(far_ref, qt_ref, k_ref, vt_ref, sel_ref, bias_ref, o_ref):
    hq = pl.program_id(1)
    i = pl.program_id(2)
    neg = -jnp.inf
    heads = range(HEADS_PER_STEP)
    ones_rows = jnp.ones((DENOM_ROWS, MOBA_BLOCK), BF16)
    dim_row = lax.broadcasted_iota(jnp.int32, (V7X_LANES, MOBA_BLOCK), 0)

    def slab_of(hh):
        slab = hh // HEADS_PER_SLAB
        return slice(slab * V7X_LANES, (slab + 1) * V7X_LANES)

    q_heads = []
    for hh in heads:
        lo = (hh % HEADS_PER_SLAB) * HEAD_DIM
        q_slab = qt_ref[0, slab_of(hh), :]
        own = (dim_row >= lo) & (dim_row < lo + HEAD_DIM)
        q_heads.append(jnp.where(own, q_slab, jnp.zeros_like(q_slab)))

    def raw_scores(hh, j):
        k_blk = k_ref[0, pl.ds(pl.multiple_of(j * MOBA_BLOCK, MOBA_BLOCK), MOBA_BLOCK), slab_of(hh)]
        return _dot(k_blk, q_heads[hh])

    def value_rows(hh, j):
        v_t = vt_ref[0, hh * HEAD_DIM:(hh + 1) * HEAD_DIM,
                     pl.ds(pl.multiple_of(j * MOBA_BLOCK, MOBA_BLOCK), MOBA_BLOCK)]
        return jnp.concatenate([v_t, ones_rows], axis=0)

    def select_row(hh, j):
        off = i - j
        row = jnp.where(off > 0, sel_ref[0, hh, pl.ds(j, 1), :], neg)
        return jnp.where(off == 0, 0.0, row)

    def far_bias(hh):
        return far_ref[hq * HEADS_PER_STEP + hh] * LOG2E

    def bias_tile(hh, j):
        off = i - j
        tile = bias_ref[hh, jnp.clip(off, 0, NEAR_OFFSETS - 1)]
        return jnp.where(off >= NEAR_OFFSETS, far_bias(hh), tile)

    def finish(accs):
        out_t = jnp.concatenate(
            [acc[:HEAD_DIM] * (1.0 / acc[HEAD_DIM:HEAD_DIM + 1]) for acc in accs], axis=0)
        return out_t.T.astype(BF16)

    def one_pass(accs, tiles, block_of, shift, ready=None):
        accs = list(accs)
        pending = []
        for t in range(len(tiles) + SCORE_LOOKAHEAD):
            if t < len(tiles):
                hh, n = tiles[t]
                if ready is not None and n == 0:
                    pending.append(ready[hh])
                else:
                    pending.append(raw_scores(hh, block_of(n)) + shift(hh, n))
            if t >= SCORE_LOOKAHEAD:
                hh, n = tiles[t - SCORE_LOOKAHEAD]
                p = jnp.exp2(pending[t - SCORE_LOOKAHEAD]).astype(BF16)
                accs[hh] = accs[hh] + _dot(value_rows(hh, block_of(n)), p)
        return tuple(accs)

    def near_pass(n_blocks):
        own = [raw_scores(hh, i) + bias_ref[hh, 0] for hh in heads]
        refs = [jnp.max(s, axis=0, keepdims=True) for s in own]

        def block_of(n):
            return jnp.maximum(i - n, 0)

        def shift(hh, n):
            row = jnp.where(i - n >= 0, sel_ref[0, hh, pl.ds(block_of(n), 1), :], neg)
            if n >= NEAR_OFFSETS:
                return row + (far_bias(hh) - refs[hh])
            return bias_ref[hh, n] + (row - refs[hh])

        zero = jnp.zeros((HEAD_DIM + DENOM_ROWS, MOBA_BLOCK), F32)
        tiles = [(hh, n) for n in range(n_blocks) for hh in heads]
        accs = one_pass((zero,) * HEADS_PER_STEP, tiles, block_of, shift,
                        [own[hh] - refs[hh] for hh in heads])
        return accs, tuple(refs)

    n_old = jnp.maximum(i - (NEAR_OFFSETS - 1), 0)
    accs, refs = lax.switch(
        n_old % KEY_GROUP,
        [functools.partial(near_pass, NEAR_OFFSETS + extra) for extra in range(KEY_GROUP)])

    def far_blocks(first_block, n_blocks, accs):
        def block_of(n):
            return first_block + n

        def shift(hh, n):
            return sel_ref[0, hh, pl.ds(block_of(n), 1), :] + (far_bias(hh) - refs[hh])

        tiles = [(hh, n) for n in range(n_blocks) for hh in heads]
        return one_pass(accs, tiles, block_of, shift)

    n_groups = n_old // KEY_GROUP
    accs = lax.fori_loop(
        0, n_groups // 2, lambda t, a: far_blocks(2 * KEY_GROUP * t, 2 * KEY_GROUP, a), accs)
    accs = lax.cond(n_groups % 2 == 1,
                    lambda a: far_blocks((n_groups - 1) * KEY_GROUP, KEY_GROUP, a),
                    lambda a: a, accs)

    def running_max_pass():
        def visit(t, states):
            j = i - t
            out = []
            for hh in heads:
                m, acc = states[hh]
                s = raw_scores(hh, j) + bias_tile(hh, j) + select_row(hh, j)
                m_new = jnp.maximum(m, jnp.max(s, axis=0, keepdims=True))
                p = jnp.exp2(s - m_new).astype(BF16)
                out.append((m_new, jnp.exp2(m - m_new) * acc + _dot(value_rows(hh, j), p)))
            return tuple(out)

        init = ((jnp.full((1, MOBA_BLOCK), neg, F32),
                 jnp.zeros((HEAD_DIM + DENOM_ROWS, MOBA_BLOCK), F32)),) * HEADS_PER_STEP
        states = lax.fori_loop(0, i + 1, visit, init)
        return finish([acc for (_, acc) in states])

    largest = functools.reduce(jnp.maximum, [jnp.max(jnp.abs(acc)) for acc in accs])
    overflowed = jnp.logical_not(largest < jnp.inf)
    o_ref[0] = finish(accs)

    @pl.when(overflowed)
    def _():
        o_ref[0] = running_max_pass()


def _attention(q_t, k, v_t, select, bias_tiles, far_bias):
    b, d, s = q_t.shape
    nb = s // MOBA_BLOCK
    width = HEADS_PER_STEP * HEAD_DIM
    return pl.pallas_call(
        _attn_kernel,
        grid=(b, d // width, nb),
        in_specs=[
            pl.BlockSpec(memory_space=pltpu.SMEM),
            pl.BlockSpec((1, width, MOBA_BLOCK), lambda bi, hq, i: (bi, hq, i)),
            pl.BlockSpec((1, s, width), lambda bi, hq, i: (bi, 0, hq)),
            pl.BlockSpec((1, width, s), lambda bi, hq, i: (bi, hq, 0)),
            pl.BlockSpec((1, HEADS_PER_STEP, nb, MOBA_BLOCK), lambda bi, hq, i: (bi, hq, 0, i)),
            pl.BlockSpec((HEADS_PER_STEP, NEAR_OFFSETS, MOBA_BLOCK, MOBA_BLOCK),
                         lambda bi, hq, i: (hq, 0, 0, 0), pipeline_mode=pl.Buffered(1)),
        ],
        out_specs=pl.BlockSpec((1, MOBA_BLOCK, width), lambda bi, hq, i: (bi, i, hq)),
        out_shape=jax.ShapeDtypeStruct((b, s, d), BF16),
        compiler_params=_params(("parallel", "parallel", "parallel")),
        name="moba_attention",
    )(far_bias, q_t, k, v_t, select, bias_tiles)


def kernel(x, c, rel_bias, w_mod, b_mod, norm_mix, norm_mlp, w_pool, pool_scale,
           w_qkv, w_o, w_up, w_down, norm_final):
    b, s, d = x.shape
    assert d % (HEADS_PER_STEP * HEAD_DIM) == 0 and s % MLP_ROWS == 0
    assert s % QKV_ROWS == 0 and QKV_ROWS % MOBA_BLOCK == 0
    assert w_mod.shape[0] == 2 and w_mod.shape[2] % MOD_COLS == 0
    mod = _modulation(c, w_mod, b_mod)
    sh1, sc1, g1, sh2, sc2, g2 = (mod[:, :, n * d:(n + 1) * d] for n in range(6))

    x = _pool_mlp_layer(x, norm_mix[0], sh1[0], sc1[0], g1[0], w_pool[0], pool_scale[0],
                        norm_mlp[0], sh2[0], sc2[0], g2[0], w_up[0], w_down[0])

    q_t, k, v_t, select = _qkv_layer(x, norm_mix[1], sh1[1], sc1[1], w_qkv[0])
    attn = _attention(q_t, k, v_t, select, _bias_tiles(rel_bias), rel_bias[NUM_BUCKETS - 1])
    return _attn_mlp_layer(x, attn, w_o[0], g1[1], norm_mlp[1], sh2[1], sc2[1], g2[1],
                           w_up[1], w_down[1], norm_final)
```

```python
import functools
import math

import jax
import jax.numpy as jnp
from jax import lax
from jax.experimental import pallas as pl
from jax.experimental.pallas import tpu as pltpu

F32 = jnp.float32
BF16 = jnp.bfloat16

HEAD_DIM = 64
MOBA_BLOCK = 256
MOBA_TOPK = 3
NUM_BUCKETS = 32
MAX_DISTANCE = 1024
POOL_WINDOWS = (2, 4, 8, 16)
EPS = 1e-6

V7X_LANES = 128
V7X_SUBLANES = 8
V7X_VMEM_LIMIT_BYTES = 56 * 1024 * 1024

NEAR_OFFSETS = 5
KEY_GROUP = 4
DENOM_ROWS = 16
SCORE_LOOKAHEAD = 8
LOG2E = math.log2(math.e)
HEADS_PER_SLAB = V7X_LANES // HEAD_DIM
HEADS_PER_STEP = 4 * HEADS_PER_SLAB
POOL_HALO = max(POOL_WINDOWS)

MLP_ROWS = 512
MLP_FF_CHUNK = 512
QKV_ROWS = 512
MOD_COLS = 1536


def _params(semantics):
    return pltpu.CompilerParams(dimension_semantics=semantics,
                                vmem_limit_bytes=V7X_VMEM_LIMIT_BYTES)


def _dot(a, b):
    return jnp.dot(a, b, preferred_element_type=F32)


def _rms_modulate(x, gain, shift, scale):
    y = x * lax.rsqrt(jnp.mean(x * x, axis=-1, keepdims=True) + EPS)
    return (y * gain) * (1.0 + scale) + shift


def _mod_kernel(c_ref, w_ref, b_ref, o_ref):
    c = c_ref[...]
    a = c * (1.0 / (1.0 + jnp.exp(-c)))
    w = w_ref[0]
    a_hi = a.astype(BF16)
    a_lo = (a - a_hi.astype(F32)).astype(BF16)
    w_hi = w.astype(BF16)
    w_lo = (w - w_hi.astype(F32)).astype(BF16)
    acc = _dot(a_hi, w_hi) + _dot(a_hi, w_lo) + _dot(a_lo, w_hi)
    o_ref[0] = acc + b_ref[0]


def _modulation(c, w_mod, b_mod):
    depth, d, n = w_mod.shape
    b = c.shape[0]
    rows = -(-b // V7X_SUBLANES) * V7X_SUBLANES
    c_pad = jnp.pad(c, ((0, rows - b), (0, 0)))
    out = pl.pallas_call(
        _mod_kernel,
        grid=(depth, n // MOD_COLS),
        in_specs=[
            pl.BlockSpec((rows, d), lambda i, j: (0, 0)),
            pl.BlockSpec((1, d, MOD_COLS), lambda i, j: (i, 0, j)),
            pl.BlockSpec((1, 1, MOD_COLS), lambda i, j: (i, 0, j)),
        ],
        out_specs=pl.BlockSpec((1, rows, MOD_COLS), lambda i, j: (i, 0, j)),
        out_shape=jax.ShapeDtypeStruct((depth, rows, n), F32),
        compiler_params=_params(("parallel", "parallel")),
        name="modulation",
    )(c_pad, w_mod, b_mod.reshape(depth, 1, n))
    return out[:, :b]


def _pool_residual(x, halo, t, gain, shift, scale, gate, w_ref, pool_scale):
    rows = x.shape[0]
    group = w_ref.shape[1]
    h = _rms_modulate(x, gain, shift, scale)
    h_prev = jnp.where(t > 0, _rms_modulate(halo, gain, shift, scale), 0.0)
    h_ext = jnp.concatenate([h_prev, h], axis=0)
    pos = t * rows + lax.broadcasted_iota(jnp.int32, (rows, 1), 0)
    ys = []
    for g, window in enumerate(POOL_WINDOWS):
        cols = slice(g * group, (g + 1) * group)
        s = h_ext[:, cols]
        step = 1
        while step < window:
            s = s + pltpu.roll(s, step, axis=0)
            step *= 2
        inv_cnt = 1.0 / jnp.minimum(pos + 1, window).astype(F32)
        pooled = s[POOL_HALO:] * inv_cnt - h[:, cols]
        ys.append(_dot(pooled.astype(BF16), w_ref[g]))
    return x + gate * (jnp.concatenate(ys, axis=-1) * pool_scale)


def _mlp_residual(x, gain, shift, scale, gate, wup_ref, wdn_ref, h_scr, acc_scr):
    h_scr[...] = _rms_modulate(x, gain, shift, scale).astype(BF16)
    acc_scr[...] = jnp.zeros_like(acc_scr)

    def chunk(f, carry):
        f0 = pl.multiple_of(f * MLP_FF_CHUNK, MLP_FF_CHUNK)
        up = _dot(h_scr[...], wup_ref[:, pl.ds(f0, MLP_FF_CHUNK)])
        a = jnp.maximum(up, 0.0)
        acc_scr[...] += _dot((a * a).astype(BF16), wdn_ref[pl.ds(f0, MLP_FF_CHUNK), :])
        return carry

    lax.fori_loop(0, wup_ref.shape[1] // MLP_FF_CHUNK, chunk, 0, unroll=True)
    return x + gate * acc_scr[...]


def _pool_mlp_kernel(x_ref, halo_ref, gain1_ref, sh1_ref, sc1_ref, g1_ref, wpool_ref, ps_ref,
                     gain2_ref, sh2_ref, sc2_ref, g2_ref, wup_ref, wdn_ref, o_ref, h_scr, acc_scr):
    x = _pool_residual(x_ref[0], halo_ref[0], pl.program_id(1), gain1_ref[...], sh1_ref[0],
                       sc1_ref[0], g1_ref[0], wpool_ref, ps_ref[...])
    o_ref[0] = _mlp_residual(x, gain2_ref[...], sh2_ref[0], sc2_ref[0], g2_ref[0],
                             wup_ref, wdn_ref, h_scr, acc_scr)


def _attn_mlp_kernel(x_ref, attn_ref, wo_ref, g1_ref, gain2_ref, sh2_ref, sc2_ref, g2_ref,
                     wup_ref, wdn_ref, fin_ref, o_ref, h_scr, acc_scr):
    proj = lax.dot_general(attn_ref[0], wo_ref[...], (((0,), (0,)), ((), ())),
                           preferred_element_type=F32)
    x = x_ref[0] + g1_ref[0] * proj
    out = _mlp_residual(x, gain2_ref[...], sh2_ref[0], sc2_ref[0], g2_ref[0],
                        wup_ref, wdn_ref, h_scr, acc_scr)
    out = out * lax.rsqrt(jnp.mean(out * out, axis=-1, keepdims=True) + EPS)
    o_ref[0] = out * fin_ref[...]


def _layer_specs(b, d):
    tile = pl.BlockSpec((1, MLP_ROWS, d), lambda i, t: (i, t, 0))
    vec = pl.BlockSpec((1, 1, d), lambda i, t: (i, 0, 0))
    row = pl.BlockSpec((1, d), lambda i, t: (0, 0))
    scratch = [pltpu.VMEM((MLP_ROWS, d), BF16), pltpu.VMEM((MLP_ROWS, d), F32)]
    return tile, vec, row, scratch


def _resident(shape):
    return pl.BlockSpec(shape, lambda i, t: (0,) * len(shape), pipeline_mode=pl.Buffered(1))


def _pool_mlp_layer(x, gain1, sh1, sc1, g1, w_pool, pool_scale, gain2, sh2, sc2, g2, w_up, w_down):
    b, s, d = x.shape
    tile, vec, row, scratch = _layer_specs(b, d)
    halo_blocks = MLP_ROWS // POOL_HALO
    halo = pl.BlockSpec((1, POOL_HALO, d), lambda i, t: (i, jnp.maximum(t * halo_blocks - 1, 0), 0))
    per_batch = lambda v: v.reshape(b, 1, d)
    return pl.pallas_call(
        _pool_mlp_kernel,
        grid=(b, s // MLP_ROWS),
        in_specs=[tile, halo, row, vec, vec, vec, _resident(w_pool.shape), row,
                  row, vec, vec, vec, _resident(w_up.shape), _resident(w_down.shape)],
        out_specs=tile,
        out_shape=jax.ShapeDtypeStruct((b, s, d), F32),
        scratch_shapes=scratch,
        compiler_params=_params(("parallel", "parallel")),
        name="pool_mlp",
    )(x, x, gain1.reshape(1, d), per_batch(sh1), per_batch(sc1), per_batch(g1),
      w_pool.astype(BF16), pool_scale.reshape(1, d),
      gain2.reshape(1, d), per_batch(sh2), per_batch(sc2), per_batch(g2),
      w_up.astype(BF16), w_down.astype(BF16))


def _attn_mlp_layer(x, attn, w_o, g1, gain2, sh2, sc2, g2, w_up, w_down, norm_final):
    b, s, d = x.shape
    tile, vec, row, scratch = _layer_specs(b, d)
    per_batch = lambda v: v.reshape(b, 1, d)
    return pl.pallas_call(
        _attn_mlp_kernel,
        grid=(b, s // MLP_ROWS),
        in_specs=[tile, pl.BlockSpec((1, d, MLP_ROWS), lambda i, t: (i, 0, t)),
                  _resident(w_o.shape), vec, row, vec, vec, vec,
                  _resident(w_up.shape), _resident(w_down.shape), row],
        out_specs=tile,
        out_shape=jax.ShapeDtypeStruct((b, s, d), F32),
        scratch_shapes=scratch,
        compiler_params=_params(("parallel", "parallel")),
        name="attn_proj_mlp_norm",
    )(x, attn, w_o.astype(BF16), per_batch(g1), gain2.reshape(1, d), per_batch(sh2),
      per_batch(sc2), per_batch(g2), w_up.astype(BF16), w_down.astype(BF16),
      norm_final.reshape(1, d))


def _qkv_kernel(x_ref, gain_ref, sh_ref, sc_ref, wk_ref, wqvt_ref,
                qt_ref, k_ref, vt_ref, sel_ref, km_ref):
    t = pl.program_id(1)
    d = x_ref.shape[2]
    rows = x_ref.shape[1]
    blocks = rows // MOBA_BLOCK
    n_blocks = km_ref.shape[0]
    neg = -jnp.inf
    h = _rms_modulate(x_ref[0], gain_ref[...], sh_ref[0], sc_ref[0]).astype(BF16)
    k = _dot(h, wk_ref[...])
    k_ref[0] = k.astype(BF16)
    k_mean = jnp.mean(k.reshape(blocks, MOBA_BLOCK, d), axis=1)

    @pl.when(t == 0)
    def _():
        km_ref[...] = jnp.zeros_like(km_ref)

    for step in range(n_blocks // blocks):
        @pl.when(t == step)
        def _():
            km_ref[step * blocks:(step + 1) * blocks, :] = k_mean

    lane = lax.broadcasted_iota(jnp.int32, (n_blocks, V7X_LANES), 1)
    blk = lax.broadcasted_iota(jnp.int32, (n_blocks, rows), 0)
    own_blk = (t * rows + lax.broadcasted_iota(jnp.int32, (n_blocks, rows), 1)) // MOBA_BLOCK
    blk_f = blk.astype(F32)

    def transposed(w_rows):
        return lax.dot_general(wqvt_ref[w_rows, :], h, (((1,), (1,)), ((), ())),
                               preferred_element_type=F32)

    q_t = (transposed(slice(0, d)) * (HEAD_DIM ** -0.5 * LOG2E)).astype(BF16)
    qt_ref[0] = q_t
    gates = []
    for head in range(d // HEAD_DIM):
        slab = slice(head // HEADS_PER_SLAB * V7X_LANES, (head // HEADS_PER_SLAB + 1) * V7X_LANES)
        lo = (head % HEADS_PER_SLAB) * HEAD_DIM
        in_head = (lane >= lo) & (lane < lo + HEAD_DIM)
        means = jnp.where(in_head, km_ref[:, slab], 0.0).astype(BF16)
        gates.append(jnp.where(blk < own_blk, _dot(means, q_t[slab]), neg))
    vt_ref[0] = transposed(slice(d, 2 * d)).astype(BF16)
    for head, gate in enumerate(gates):
        chosen = jnp.full((n_blocks, rows), neg, F32)
        for _ in range(MOBA_TOPK):
            best = jnp.max(gate, axis=0, keepdims=True)
            first = jnp.min(jnp.where(gate == best, blk_f, float(n_blocks)), axis=0, keepdims=True)
            hit = blk_f == first
            chosen = jnp.where(hit & (best > neg), 0.0, chosen)
            gate = jnp.where(hit, neg, gate)
        sel_ref[0, head] = chosen


def _qkv_layer(x, gain, shift, scale, w_qkv):
    b, s, d = x.shape
    nb = s // MOBA_BLOCK
    heads = d // HEAD_DIM
    vec = pl.BlockSpec((1, 1, d), lambda i, t: (i, 0, 0))
    row = pl.BlockSpec((1, d), lambda i, t: (0, 0))
    resident = dict(pipeline_mode=pl.Buffered(1))
    w_k = w_qkv[:, d:2 * d].astype(BF16)
    w_qvt = jnp.concatenate([w_qkv[:, :d], w_qkv[:, 2 * d:]], axis=1).T.astype(BF16)
    transposed = pl.BlockSpec((1, d, QKV_ROWS), lambda i, t: (i, 0, t))
    return pl.pallas_call(
        _qkv_kernel,
        grid=(b, s // QKV_ROWS),
        in_specs=[
            pl.BlockSpec((1, QKV_ROWS, d), lambda i, t: (i, t, 0)),
            row, vec, vec,
            pl.BlockSpec((d, d), lambda i, t: (0, 0), **resident),
            pl.BlockSpec((2 * d, d), lambda i, t: (0, 0), **resident),
        ],
        out_specs=[
            transposed,
            pl.BlockSpec((1, QKV_ROWS, d), lambda i, t: (i, t, 0)),
            transposed,
            pl.BlockSpec((1, heads, nb, QKV_ROWS), lambda i, t: (i, 0, 0, t)),
        ],
        out_shape=[
            jax.ShapeDtypeStruct((b, d, s), BF16),
            jax.ShapeDtypeStruct((b, s, d), BF16),
            jax.ShapeDtypeStruct((b, d, s), BF16),
            jax.ShapeDtypeStruct((b, heads, nb, s), F32),
        ],
        scratch_shapes=[pltpu.VMEM((nb, d), F32)],
        compiler_params=_params(("parallel", "arbitrary")),
        name="qkv_select",
    )(x, gain.reshape(1, d), shift.reshape(b, 1, d), scale.reshape(b, 1, d), w_k, w_qvt)


def _bias_kernel(rb_ref, o_ref):
    head = pl.program_id(0)
    span = 2 * MOBA_BLOCK
    max_exact = NUM_BUCKETS // 2
    for off in range(NEAR_OFFSETS):
        dist = MOBA_BLOCK * off - (MOBA_BLOCK - 1) + lax.broadcasted_iota(jnp.int32, (1, span), 1)
        dpos = jnp.maximum(dist, 0)
        nf = jnp.maximum(dpos, 1).astype(F32)
        large = max_exact + (jnp.log(nf / max_exact) / math.log(MAX_DISTANCE / max_exact)
                             * (NUM_BUCKETS - max_exact)).astype(jnp.int32)
        large = jnp.minimum(large, NUM_BUCKETS - 1)
        bucket = jnp.where(dpos < max_exact, dpos, large)
        val = jnp.zeros((1, span), F32)
        for bkt in range(NUM_BUCKETS):
            val = jnp.where(bucket == bkt, rb_ref[bkt, head], val)
        val = val * LOG2E
        if off == 0:
            val = jnp.where(dist >= 0, val, -jnp.inf)
        rolled = pltpu.roll(jnp.broadcast_to(val, (MOBA_BLOCK, span)), MOBA_BLOCK + 1, 1,
                            stride=1, stride_axis=0)
        o_ref[0, off] = rolled[:, :MOBA_BLOCK]


def _bias_tiles(rel_bias):
    heads = rel_bias.shape[1]
    return pl.pallas_call(
        _bias_kernel,
        grid=(heads,),
        in_specs=[pl.BlockSpec(memory_space=pltpu.SMEM)],
        out_specs=pl.BlockSpec((1, NEAR_OFFSETS, MOBA_BLOCK, MOBA_BLOCK),
                               lambda h: (h, 0, 0, 0)),
        out_shape=jax.ShapeDtypeStruct((heads, NEAR_OFFSETS, MOBA_BLOCK, MOBA_BLOCK), F32),
        compiler_params=_params(("parallel",)),
        name="t5_bias_tiles",
    )(rel_bias)


def _attn_kernel(far_ref, qt_ref, k_ref, vt_ref, sel_ref, bias_ref, o_ref):
    hq = pl.program_id(1)
    i = pl.program_id(2)
    neg = -jnp.inf
    heads = range(HEADS_PER_STEP)
    ones_rows = jnp.ones((DENOM_ROWS, MOBA_BLOCK), BF16)
    dim_row = lax.broadcasted_iota(jnp.int32, (V7X_LANES, MOBA_BLOCK), 0)

    def slab_of(hh):
        slab = hh // HEADS_PER_SLAB
        return slice(slab * V7X_LANES, (slab + 1) * V7X_LANES)

    q_heads = []
    for hh in heads:
        lo = (hh % HEADS_PER_SLAB) * HEAD_DIM
        q_slab = qt_ref[0, slab_of(hh), :]
        own = (dim_row >= lo) & (dim_row < lo + HEAD_DIM)
        q_heads.append(jnp.where(own, q_slab, jnp.zeros_like(q_slab)))

    def raw_scores(hh, j):
        k_blk = k_ref[0, pl.ds(pl.multiple_of(j * MOBA_BLOCK, MOBA_BLOCK), MOBA_BLOCK), slab_of(hh)]
        return _dot(k_blk, q_heads[hh])

    def value_rows(hh, j):
        v_t = vt_ref[0, hh * HEAD_DIM:(hh + 1) * HEAD_DIM,
                     pl.ds(pl.multiple_of(j * MOBA_BLOCK, MOBA_BLOCK), MOBA_BLOCK)]
        return jnp.concatenate([v_t, ones_rows], axis=0)

    def select_row(hh, j):
        off = i - j
        row = jnp.where(off > 0, sel_ref[0, hh, pl.ds(j, 1), :], neg)
        return jnp.where(off == 0, 0.0, row)

    def far_bias(hh):
        return far_ref[hq * HEADS_PER_STEP + hh] * LOG2E

    def bias_tile(hh, j):
        off = i - j
        tile = bias_ref[hh, jnp.clip(off, 0, NEAR_OFFSETS - 1)]
        return jnp.where(off >= NEAR_OFFSETS, far_bias(hh), tile)

    def finish(accs):
        out_t = jnp.concatenate(
            [acc[:HEAD_DIM] * (1.0 / acc[HEAD_DIM:HEAD_DIM + 1]) for acc in accs], axis=0)
        return out_t.astype(BF16)

    def one_pass(accs, tiles, block_of, shift, ready=None):
        accs = list(accs)
        pending = []
        for t in range(len(tiles) + SCORE_LOOKAHEAD):
            if t < len(tiles):
                hh, n = tiles[t]
                if ready is not None and n == 0:
                    pending.append(ready[hh])
                else:
                    pending.append(raw_scores(hh, block_of(n)) + shift(hh, n))
            if t >= SCORE_LOOKAHEAD:
                hh, n = tiles[t - SCORE_LOOKAHEAD]
                p = jnp.exp2(pending[t - SCORE_LOOKAHEAD]).astype(BF16)
                accs[hh] = accs[hh] + _dot(value_rows(hh, block_of(n)), p)
        return tuple(accs)

    def near_pass(n_blocks):
        own = [raw_scores(hh, i) + bias_ref[hh, 0] for hh in heads]
        refs = [jnp.max(s, axis=0, keepdims=True) for s in own]

        def block_of(n):
            return jnp.maximum(i - n, 0)

        def shift(hh, n):
            row = jnp.where(i - n >= 0, sel_ref[0, hh, pl.ds(block_of(n), 1), :], neg)
            if n >= NEAR_OFFSETS:
                return row + (far_bias(hh) - refs[hh])
            return bias_ref[hh, n] + (row - refs[hh])

        zero = jnp.zeros((HEAD_DIM + DENOM_ROWS, MOBA_BLOCK), F32)
        tiles = [(hh, n) for n in range(n_blocks) for hh in heads]
        accs = one_pass((zero,) * HEADS_PER_STEP, tiles, block_of, shift,
                        [own[hh] - refs[hh] for hh in heads])
        return accs, tuple(refs)

    n_old = jnp.maximum(i - (NEAR_OFFSETS - 1), 0)
    accs, refs = lax.switch(
        n_old % KEY_GROUP,
        [functools.partial(near_pass, NEAR_OFFSETS + extra) for extra in range(KEY_GROUP)])

    def far_blocks(first_block, n_blocks, accs):
        def block_of(n):
            return first_block + n

        def shift(hh, n):
            return sel_ref[0, hh, pl.ds(block_of(n), 1), :] + (far_bias(hh) - refs[hh])

        tiles = [(hh, n) for n in range(n_blocks) for hh in heads]
        return one_pass(accs, tiles, block_of, shift)

    n_groups = n_old // KEY_GROUP
    accs = lax.fori_loop(
        0, n_groups // 2, lambda t, a: far_blocks(2 * KEY_GROUP * t, 2 * KEY_GROUP, a), accs)
    accs = lax.cond(n_groups % 2 == 1,
                    lambda a: far_blocks((n_groups - 1) * KEY_GROUP, KEY_GROUP, a),
                    lambda a: a, accs)

    def running_max_pass():
        def visit(t, states):
            j = i - t
            out = []
            for hh in heads:
                m, acc = states[hh]
                s = raw_scores(hh, j) + bias_tile(hh, j) + select_row(hh, j)
                m_new = jnp.maximum(m, jnp.max(s, axis=0, keepdims=True))
                p = jnp.exp2(s - m_new).astype(BF16)
                out.append((m_new, jnp.exp2(m - m_new) * acc + _dot(value_rows(hh, j), p)))
            return tuple(out)

        init = ((jnp.full((1, MOBA_BLOCK), neg, F32),
                 jnp.zeros((HEAD_DIM + DENOM_ROWS, MOBA_BLOCK), F32)),) * HEADS_PER_STEP
        states = lax.fori_loop(0, i + 1, visit, init)
        return finish([acc for (_, acc) in states])

    largest = functools.reduce(jnp.maximum, [jnp.max(jnp.abs(acc)) for acc in accs])
    overflowed = jnp.logical_not(largest < jnp.inf)
    o_ref[0] = finish(accs)

    @pl.when(overflowed)
    def _():
        o_ref[0] = running_max_pass()


def _attention(q_t, k, v_t, select, bias_tiles, far_bias):
    b, d, s = q_t.shape
    nb = s // MOBA_BLOCK
    width = HEADS_PER_STEP * HEAD_DIM
    return pl.pallas_call(
        _attn_kernel,
        grid=(b, d // width, nb),
        in_specs=[
            pl.BlockSpec(memory_space=pltpu.SMEM),
            pl.BlockSpec((1, width, MOBA_BLOCK), lambda bi, hq, i: (bi, hq, i)),
            pl.BlockSpec((1, s, width), lambda bi, hq, i: (bi, 0, hq)),
            pl.BlockSpec((1, width, s), lambda bi, hq, i: (bi, hq, 0)),
            pl.BlockSpec((1, HEADS_PER_STEP, nb, MOBA_BLOCK), lambda bi, hq, i: (bi, hq, 0, i)),
            pl.BlockSpec((HEADS_PER_STEP, NEAR_OFFSETS, MOBA_BLOCK, MOBA_BLOCK),
                         lambda bi, hq, i: (hq, 0, 0, 0), pipeline_mode=pl.Buffered(1)),
        ],
        out_specs=pl.BlockSpec((1, width, MOBA_BLOCK), lambda bi, hq, i: (bi, hq, i)),
        out_shape=jax.ShapeDtypeStruct((b, d, s), BF16),
        compiler_params=_params(("parallel", "parallel", "parallel")),
        name="moba_attention",
    )(far_bias, q_t, k, v_t, select, bias_tiles)


def kernel(x, c, rel_bias, w_mod, b_mod, norm_mix, norm_mlp, w_pool, pool_scale,
           w_qkv, w_o, w_up, w_down, norm_final):
    b, s, d = x.shape
    assert d % (HEADS_PER_STEP * HEAD_DIM) == 0 and s % MLP_ROWS == 0
    assert s % QKV_ROWS == 0 and QKV_ROWS % MOBA_BLOCK == 0
    assert w_mod.shape[0] == 2 and w_mod.shape[2] % MOD_COLS == 0
    mod = _modulation(c, w_mod, b_mod)
    sh1, sc1, g1, sh2, sc2, g2 = (mod[:, :, n * d:(n + 1) * d] for n in range(6))

    x = _pool_mlp_layer(x, norm_mix[0], sh1[0], sc1[0], g1[0], w_pool[0], pool_scale[0],
                        norm_mlp[0], sh2[0], sc2[0], g2[0], w_up[0], w_down[0])

    q_t, k, v_t, select = _qkv_layer(x, norm_mix[1], sh1[1], sc1[1], w_qkv[0])
    attn = _attention(q_t, k, v_t, select, _bias_tiles(rel_bias), rel_bias[NUM_BUCKETS - 1])
    return _attn_mlp_layer(x, attn, w_o[0], g1[1], norm_mlp[1], sh2[1], sc2[1], g2[1],
                           w_up[1], w_down[1], norm_final)
```
